```python
import jax, jax.numpy as jnp
from jax import lax
import numpy as np


D_MODEL = 1024
BATCH = 32
SEQ = 2048
DEPTH = 4

GRID_W = 64
CTX_LEN = 256
N_MIXERS = 2
N_FOURIER_GROUPS = 4
FOURIER_GROUP = D_MODEL // N_FOURIER_GROUPS
D_RNN = (4 * D_MODEL // 3) // 64 * 64
N_RNN_BLOCKS = 16
RNN_BLOCK = D_RNN // N_RNN_BLOCKS
CONV_W = 4
CONV_LEFT = CONV_W // 2
RG_C = 8.0
D_FF = 4 * D_MODEL
N_FOURIER_LAYERS = (DEPTH + 1) // 2
N_RNN_LAYERS = DEPTH // 2
EPS = 1e-6

kernel_name = 'hybrid_fnet_rglru_prefix_dit'


def rmsnorm(x, g):
    xf = x.astype(jnp.float32)
    y = xf * lax.rsqrt(jnp.mean(xf * xf, axis=-1, keepdims=True) + EPS)
    return (y * g.astype(jnp.float32)).astype(x.dtype)


def modulate(h, shift, scale):
    return h * (1 + scale) + shift


def fourier_mix(h, w_out):
    B, T, D = h.shape
    hg = h.astype(jnp.float32).reshape(B, T, N_FOURIER_GROUPS, FOURIER_GROUP)
    f = jnp.fft.fftn(hg, axes=(1, 3), norm='ortho').real
    return f.reshape(B, T, D).astype(h.dtype) @ w_out


def centred_dwconv(u, w, b):
    T = u.shape[1]
    up = jnp.pad(u, ((0, 0), (CONV_LEFT, CONV_W - 1 - CONV_LEFT), (0, 0)))
    out = b
    for k in range(CONV_W):
        out = out + w[k] * up[:, k:k + T]
    return out


def _combine(l, r):
    a_l, b_l = l
    a_r, b_r = r
    return a_l * a_r, a_r * b_l + b_r


def linear_recurrence(a, b, h0, reverse):
    if reverse:
        a, b = jnp.flip(a, 1), jnp.flip(b, 1)
    if h0 is not None:
        b = b.at[:, 0].add(a[:, 0] * h0)
    _, h = lax.associative_scan(_combine, (a, b), axis=1)
    if reverse:
        h = jnp.flip(h, 1)
    return h


def rglru_direction(xr, w_a, b_a, w_i, b_i, lam, h0, reverse):
    B, T, _ = xr.shape
    xb = xr.reshape(B, T, N_RNN_BLOCKS, RNN_BLOCK)
    r = jax.nn.sigmoid(jnp.einsum('btni,nij->btnj', xb, w_a).reshape(B, T, D_RNN) + b_a)
    ig = jax.nn.sigmoid(jnp.einsum('btni,nij->btnj', xb, w_i).reshape(B, T, D_RNN) + b_i)
    log_a = (-RG_C * r.astype(jnp.float32)) * jax.nn.softplus(-lam.astype(jnp.float32))
    a = jnp.exp(log_a)
    mult = jnp.sqrt(-jnp.expm1(2.0 * log_a))
    bterm = mult * (ig * xr).astype(jnp.float32)
    return linear_recurrence(a, bterm, h0, reverse)


def rglru_branch(h, w_in, conv_w, conv_b, w_a, b_a, w_i, b_i, lam, h0_f, h0_b):
    u = h @ w_in
    gate, xr = jnp.split(u, 2, axis=-1)
    xr = centred_dwconv(xr, conv_w, conv_b)
    hf = rglru_direction(xr, w_a[0], b_a[0], w_i[0], b_i[0], lam[0], h0_f, False)
    hb = rglru_direction(xr, w_a[1], b_a[1], w_i[1], b_i[1], lam[1], h0_b, True)
    return hf, hb, gate


def rglru_readout(hf, hb, gate, w_out):
    return ((hf + hb).astype(gate.dtype) * jax.nn.gelu(gate)) @ w_out


def sq_relu_mlp(h, w1, b1, w2, b2):
    return jnp.square(jax.nn.relu(h @ w1 + b1)) @ w2 + b2


def setup_inputs(seed: int = 0) -> dict:
    key = jax.random.key(seed)
    ks = jax.random.split(key, 24)
    D = D_MODEL
    nr, nf = N_RNN_LAYERS, N_FOURIER_LAYERS
    f32 = jnp.float32
    nrm = lambda k, shape, s: jax.random.normal(k, shape, f32) * s
    a0 = jax.random.uniform(ks[15], (nr, 2, D_RNN), f32, minval=0.9, maxval=0.999)
    s = a0 ** (1.0 / RG_C)
    lam = jnp.log(s) - jnp.log1p(-s)
    return {
        'x': nrm(ks[0], (BATCH, SEQ, D), 1.0),
        'c': nrm(ks[1], (BATCH, D), 1.0),
        'ctx': nrm(ks[2], (BATCH, CTX_LEN, D), 1.0),
        'c_ctx': nrm(ks[3], (D,), 1.0),
        'w_mod': nrm(ks[4], (DEPTH, D, 6 * D), 0.5 * D ** -0.5),
        'b_mod': nrm(ks[5], (DEPTH, 6 * D), 0.02),
        'norm_g': 1.0 + nrm(ks[6], (DEPTH, 2, D), 0.02),
        'w_fourier': nrm(ks[7], (nf, D, D), D ** -0.5),
        'w_rnn_in': nrm(ks[8], (nr, D, 2 * D_RNN), D ** -0.5),
        'conv_w': nrm(ks[9], (nr, CONV_W, D_RNN), CONV_W ** -0.5),
        'conv_b': nrm(ks[10], (nr, D_RNN), 0.02),
        'w_a': nrm(ks[11], (nr, 2, N_RNN_BLOCKS, RNN_BLOCK, RNN_BLOCK), RNN_BLOCK ** -0.5),
        'b_a': nrm(ks[12], (nr, 2, D_RNN), 0.02),
        'w_i': nrm(ks[13], (nr, 2, N_RNN_BLOCKS, RNN_BLOCK, RNN_BLOCK), RNN_BLOCK ** -0.5),
        'b_i': nrm(ks[14], (nr, 2, D_RNN), 0.02),
        'lam': lam,
        'w_rnn_out': nrm(ks[16], (nr, D_RNN, D), D_RNN ** -0.5),
        'w1': nrm(ks[17], (DEPTH, D, D_FF), D ** -0.5),
        'b1': nrm(ks[18], (DEPTH, D_FF), 0.02),
        'w2': nrm(ks[19], (DEPTH, D_FF, D), D_FF ** -0.5),
        'b2': nrm(ks[20], (DEPTH, D), 0.02),
        'final_g': 1.0 + nrm(ks[21], (D,), 0.02),
    }


def reference(x, c, ctx, c_ctx, w_mod, b_mod, norm_g, w_fourier, w_rnn_in, conv_w, conv_b,
              w_a, b_a, w_i, b_i, lam, w_rnn_out, w1, b1, w2, b2, final_g):
    s_c = jax.nn.silu(c)
    s_cc = jax.nn.silu(c_ctx)
    for i in range(DEPTH):
        last = i == DEPTH - 1
        j = i // N_MIXERS
        mod_x = (s_c @ w_mod[i] + b_mod[i])[:, None, :]
        mod_c = (s_cc @ w_mod[i] + b_mod[i])[None, None, :]
        shx, scx, gx, shx2, scx2, gx2 = jnp.split(mod_x, 6, axis=-1)
        shc, scc, gc, shc2, scc2, gc2 = jnp.split(mod_c, 6, axis=-1)
        hx = modulate(rmsnorm(x, norm_g[i, 0]), shx, scx)
        if i % N_MIXERS == 0:
            yx = fourier_mix(hx, w_fourier[j])
            if not last:
                hc = modulate(rmsnorm(ctx, norm_g[i, 0]), shc, scc)
                yc = fourier_mix(hc, w_fourier[j])
        else:
            hc = modulate(rmsnorm(ctx, norm_g[i, 0]), shc, scc)
            p = (w_rnn_in[j], conv_w[j], conv_b[j], w_a[j], b_a[j], w_i[j], b_i[j], lam[j])
            hf_c, hb_c, gate_c = rglru_branch(hc, *p, None, None)
            hf_x, hb_x, gate_x = rglru_branch(hx, *p, hf_c[:, -1], hb_c[:, 0])
            yx = rglru_readout(hf_x, hb_x, gate_x, w_rnn_out[j])
            if not last:
                yc = rglru_readout(hf_c, hb_c, gate_c, w_rnn_out[j])
        x = x + gx * yx
        x = x + gx2 * sq_relu_mlp(modulate(rmsnorm(x, norm_g[i, 1]), shx2, scx2), w1[i], b1[i], w2[i], b2[i])
        if not last:
            ctx = ctx + gc * yc
            ctx = ctx + gc2 * sq_relu_mlp(modulate(rmsnorm(ctx, norm_g[i, 1]), shc2, scc2), w1[i], b1[i], w2[i], b2[i])
    return rmsnorm(x, final_g)
```

```python
import functools
import math

import numpy as np
import jax
import jax.numpy as jnp
from jax import lax
from jax.experimental import pallas as pl
from jax.experimental.pallas import tpu as pltpu

EPS = 1e-6
RG_C = 8.0
N_FOURIER_GROUPS = 4
CONV_W = 4
N_MOD = 6

LANES = 128
SUBLANES = 8
MXU_DIM = 256
VMEM_LIMIT = 56 << 20
_MXU_DTYPE = jnp.bfloat16
F32 = jnp.float32


def _params(*sem):
    return pltpu.CompilerParams(dimension_semantics=sem, vmem_limit_bytes=VMEM_LIMIT)


def _resident(shape, index_map):
    return pl.BlockSpec(shape, index_map, pipeline_mode=pl.Buffered(1))


def _dot(a, b):
    return jnp.dot(a, b, preferred_element_type=F32)


def _norm_mod(x, g, shift, scale):
    ms = jnp.mean(x * x, axis=-1, keepdims=True)
    y = x * lax.rsqrt(ms + EPS) * g
    return y * (1.0 + scale) + shift


def _mod_kernel(cv_ref, w_ref, b_ref, o_ref):
    c = cv_ref[...]
    s = c * jax.nn.sigmoid(c)
    o_ref[...] = _dot(s.astype(_MXU_DTYPE), w_ref[...].astype(_MXU_DTYPE)) + b_ref[...]


def _modulation(cv, w_mod, b_mod):
    depth, d, n = w_mod.shape
    rows = cv.shape[0]
    tn = min(n, 1024)
    return pl.pallas_call(
        _mod_kernel,
        grid=(depth, n // tn),
        in_specs=[
            pl.BlockSpec((rows, d), lambda i, j: (0, 0)),
            pl.BlockSpec((None, d, tn), lambda i, j: (i, 0, j)),
            pl.BlockSpec((None, 1, tn), lambda i, j: (i, 0, j)),
        ],
        out_specs=pl.BlockSpec((None, rows, tn), lambda i, j: (i, 0, j)),
        out_shape=jax.ShapeDtypeStruct((depth, rows, n), F32),
        compiler_params=_params("parallel", "parallel"),
        name="modulation",
    )(cv, w_mod, b_mod.reshape(depth, 1, n))


def _mlp_kernel(x_ref, mod_ref, g_ref, w1_ref, b1_ref, w2_ref, b2_ref, *rest, tf, final):
    if final:
        fg_ref, o_ref = rest
    else:
        (o_ref,) = rest
    x = x_ref[...]
    h = _norm_mod(x, g_ref[...], mod_ref[3:4, :], mod_ref[4:5, :]).astype(_MXU_DTYPE)
    dff = w1_ref.shape[1]
    acc = jnp.zeros(x.shape, F32)
    for j in range(dff // tf):
        a = _dot(h, w1_ref[:, j * tf:(j + 1) * tf]) + b1_ref[:, j * tf:(j + 1) * tf]
        a = jnp.maximum(a, 0.0)
        a = a * a
        acc = acc + _dot(a.astype(_MXU_DTYPE), w2_ref[j * tf:(j + 1) * tf, :])
    out = x + mod_ref[5:6, :] * (acc + b2_ref[...])
    if final:
        ms = jnp.mean(out * out, axis=-1, keepdims=True)
        out = out * lax.rsqrt(ms + EPS) * fg_ref[...]
    o_ref[...] = out


def _mlp(xa, mod, g, w1, b1, w2, b2, final_g=None):
    bsz, t, d = xa.shape
    dff = w1.shape[1]
    tm = min(t, 512)
    tf = min(dff, 1024)
    final = final_g is not None
    in_specs = [
        pl.BlockSpec((None, tm, d), lambda b, i: (b, i, 0)),
        pl.BlockSpec((None, N_MOD, d), lambda b, i: (b, 0, 0)),
        pl.BlockSpec((1, d), lambda b, i: (0, 0)),
        _resident((d, dff), lambda b, i: (0, 0)),
        pl.BlockSpec((1, dff), lambda b, i: (0, 0)),
        _resident((dff, d), lambda b, i: (0, 0)),
        pl.BlockSpec((1, d), lambda b, i: (0, 0)),
    ]
    args = [xa, mod, g.reshape(1, d), w1, b1.reshape(1, dff), w2, b2.reshape(1, d)]
    if final:
        in_specs.append(pl.BlockSpec((1, d), lambda b, i: (0, 0)))
        args.append(final_g.reshape(1, d))
    return pl.pallas_call(
        functools.partial(_mlp_kernel, tf=tf, final=final),
        grid=(bsz, t // tm),
        in_specs=in_specs,
        out_specs=pl.BlockSpec((None, tm, d), lambda b, i: (b, i, 0)),
        out_shape=jax.ShapeDtypeStruct(xa.shape, F32),
        compiler_params=_params("parallel", "parallel"),
        name="mlp",
    )(*args)


def _dft_tables(t, gs):
    c = np.arange(gs, dtype=np.int64)
    ang = 2.0 * np.pi * ((c[:, None] * c[None, :]) % gs) / gs
    wg = np.concatenate([np.cos(ang), -np.sin(ang)], axis=1)
    k = np.arange(t, dtype=np.int64)
    angt = 2.0 * np.pi * ((k[:, None] * k[None, :]) % t) / t
    cs = np.stack([np.cos(angt), np.sin(angt)])
    return jnp.asarray(wg, _MXU_DTYPE), jnp.asarray(cs, _MXU_DTYPE)


def _f1_kernel(x_ref, mod_ref, g_ref, wg_ref, o_ref, *, gs):
    h = _norm_mod(x_ref[...], g_ref[...], mod_ref[0:1, :], mod_ref[1:2, :]).astype(_MXU_DTYPE)
    for q in range(h.shape[-1] // gs):
        res = _dot(h[:, q * gs:(q + 1) * gs], wg_ref[...])
        o_ref[0, :, q * gs:(q + 1) * gs] = res[:, :gs].astype(o_ref.dtype)
        o_ref[1, :, q * gs:(q + 1) * gs] = res[:, gs:].astype(o_ref.dtype)


def _f2_kernel(x_ref, g2_ref, cs_ref, mod_ref, wo_ref, o_ref, *, tm, scale):
    r0 = pl.multiple_of(pl.program_id(1) * tm, tm)
    y = _dot(cs_ref[0, pl.ds(r0, tm), :], g2_ref[0]) + _dot(cs_ref[1, pl.ds(r0, tm), :], g2_ref[1])
    yx = _dot((y * scale).astype(_MXU_DTYPE), wo_ref[...])
    o_ref[...] = x_ref[...] + mod_ref[2:3, :] * yx


def _fourier_layer(xa, mod, g, w_out):
    bsz, t, d = xa.shape
    gs = d // N_FOURIER_GROUPS
    tm = min(t, 512)
    wg, cs = _dft_tables(t, gs)
    g2 = pl.pallas_call(
        functools.partial(_f1_kernel, gs=gs),
        grid=(bsz, t // tm),
        in_specs=[
            pl.BlockSpec((None, tm, d), lambda b, i: (b, i, 0)),
            pl.BlockSpec((None, N_MOD, d), lambda b, i: (b, 0, 0)),
            pl.BlockSpec((1, d), lambda b, i: (0, 0)),
            pl.BlockSpec((gs, 2 * gs), lambda b, i: (0, 0)),
        ],
        out_specs=pl.BlockSpec((None, 2, tm, d), lambda b, i: (b, 0, i, 0)),
        out_shape=jax.ShapeDtypeStruct((bsz, 2, t, d), _MXU_DTYPE),
        compiler_params=_params("parallel", "parallel"),
        name="fourier_group_dft",
    )(xa, mod, g.reshape(1, d), wg)
    scale = 1.0 / math.sqrt(t * gs)
    return pl.pallas_call(
        functools.partial(_f2_kernel, tm=tm, scale=scale),
        grid=(bsz, t // tm),
        in_specs=[
            pl.BlockSpec((None, tm, d), lambda b, i: (b, i, 0)),
            pl.BlockSpec((None, 2, t, d), lambda b, i: (b, 0, 0, 0)),
            _resident((2, t, t), lambda b, i: (0, 0, 0)),
            pl.BlockSpec((None, N_MOD, d), lambda b, i: (b, 0, 0)),
            _resident((d, d), lambda b, i: (0, 0)),
        ],
        out_specs=pl.BlockSpec((None, tm, d), lambda b, i: (b, i, 0)),
        out_shape=jax.ShapeDtypeStruct(xa.shape, F32),
        compiler_params=_params("parallel", "arbitrary"),
        name="fourier_time_dft",
    )(xa, g2, cs, mod, w_out)


class _RnnLayout:
    def __init__(self, n_blocks, rb):
        self.rb = rb
        self.n_blocks = n_blocks
        self.per_group = MXU_DIM // rb
        self.n_groups = -(-n_blocks // self.per_group)
        self.p = self.n_groups * MXU_DIM

    def pad_last(self, v):
        pg, rb = self.per_group, self.rb
        lead = v.shape[:-1]
        nl = len(lead)
        vb = v.reshape(lead + (self.n_blocks, rb))
        vb = jnp.pad(vb, [(0, 0)] * nl + [(0, self.n_groups * pg - self.n_blocks), (0, 0)])
        vg = vb.reshape(lead + (self.n_groups, pg * rb))
        vg = jnp.pad(vg, [(0, 0)] * nl + [(0, 0), (0, MXU_DIM - pg * rb)])
        return vg.reshape(lead + (self.p,))

    def pad_rows(self, w):
        return self.pad_last(w.T).T

    def block_diag(self, w):
        pg, rb = self.per_group, self.rb
        tot = self.n_groups * pg
        wp = jnp.pad(w, ((0, tot - self.n_blocks), (0, 0), (0, 0))).reshape(self.n_groups, pg, rb, 1, rb)
        eye = jnp.eye(pg, dtype=w.dtype).reshape(1, pg, 1, pg, 1)
        m = (wp * eye).reshape(self.n_groups, pg * rb, pg * rb)
        extra = MXU_DIM - pg * rb
        return jnp.pad(m, ((0, 0), (0, extra), (0, extra)))


def _r1_kernel(x_ref, mod_ref, g_ref, win_ref, gate_ref, xr_ref, *, tt, p):
    x = x_ref[...]
    m = mod_ref[...]
    h = _norm_mod(x, g_ref[...], m[:, 0:1, :], m[:, 1:2, :])
    hb = h.reshape(SUBLANES * tt, h.shape[-1]).astype(_MXU_DTYPE)
    u = _dot(hb, win_ref[...])
    gate_ref[...] = u[:, :p].reshape(SUBLANES, tt, p)
    for b in range(SUBLANES):
        for j in range(p // LANES):
            xr_ref[j, pl.ds(b, tt, stride=SUBLANES), :] = u[b * tt:(b + 1) * tt, p + j * LANES:p + (j + 1) * LANES]


def _rnn_in(xa, mod, g, win_p, p):
    bsz, t, d = xa.shape
    ng = bsz // SUBLANES
    tt = min(t, 64)
    ns = p // LANES
    return pl.pallas_call(
        functools.partial(_r1_kernel, tt=tt, p=p),
        grid=(ng, t // tt),
        in_specs=[
            pl.BlockSpec((SUBLANES, tt, d), lambda gi, i: (gi, i, 0)),
            pl.BlockSpec((SUBLANES, N_MOD, d), lambda gi, i: (gi, 0, 0)),
            pl.BlockSpec((1, d), lambda gi, i: (0, 0)),
            _resident((d, 2 * p), lambda gi, i: (0, 0)),
        ],
        out_specs=[
            pl.BlockSpec((SUBLANES, tt, p), lambda gi, i: (gi, i, 0)),
            pl.BlockSpec((None, ns, tt * SUBLANES, LANES), lambda gi, i: (gi, 0, i, 0)),
        ],
        out_shape=[
            jax.ShapeDtypeStruct((bsz, t, p), F32),
            jax.ShapeDtypeStruct((ng, ns, t * SUBLANES, LANES), F32),
        ],
        compiler_params=_params("parallel", "parallel"),
        name="rnn_in_proj",
    )(xa, mod, g.reshape(1, d), win_p)


def _softplus(z):
    return jnp.maximum(z, 0.0) + jnp.log(1.0 + jnp.exp(-jnp.abs(z)))


def _r2_kernel(xm_ref, xp_ref, xn_ref, cw_ref, cb_ref, wg_ref, bg_ref, lam_ref, h0f_ref, h0b_ref,
               hs_ref, hff_ref, hfb_ref, ext_ref, a_ref, b_ref, hfall_ref, st_ref, *, tt, n_t):
    k = pl.program_id(2)
    fwd = k < n_t
    c = jnp.where(fwd, k, 2 * n_t - 1 - k)
    rows = tt * SUBLANES
    halo_l = (CONV_W // 2) * SUBLANES
    halo_r = (CONV_W - 1 - CONV_W // 2) * SUBLANES
    gw = MXU_DIM

    for s in range(gw // LANES):
        ls = slice(s * LANES, (s + 1) * LANES)
        ext_ref[0:halo_l, ls] = jnp.where(c > 0, xp_ref[s], 0.0)
        ext_ref[halo_l:halo_l + rows, ls] = xm_ref[s]
        ext_ref[halo_l + rows:halo_l + rows + halo_r, ls] = jnp.where(c < n_t - 1, xn_ref[s], 0.0)

    xc = cb_ref[...] + sum(
        cw_ref[kk:kk + 1, :] * ext_ref[kk * SUBLANES:kk * SUBLANES + rows, :] for kk in range(CONV_W))
    pre = _dot(xc.astype(_MXU_DTYPE), wg_ref[...]) + bg_ref[...]
    r = jax.nn.sigmoid(pre[:, :gw])
    ig = jax.nn.sigmoid(pre[:, gw:])
    log_a = (-RG_C * r) * _softplus(-lam_ref[...])
    a = jnp.exp(log_a)
    a_ref[...] = a
    b_ref[...] = jnp.sqrt(1.0 - a * a) * (ig * xc)

    base = c * rows

    @pl.when(fwd)
    def _():
        @pl.when(c == 0)
        def _():
            st_ref[...] = h0f_ref[...]

        def body(t, h):
            o = pl.multiple_of(t * SUBLANES, SUBLANES)
            h = a_ref[pl.ds(o, SUBLANES), :] * h + b_ref[pl.ds(o, SUBLANES), :]
            hfall_ref[pl.ds(pl.multiple_of(base + o, SUBLANES), SUBLANES), :] = h
            return h

        h = lax.fori_loop(0, tt, body, st_ref[...], unroll=8)
        st_ref[...] = h

        @pl.when(c == n_t - 1)
        def _():
            hff_ref[...] = h

    @pl.when(jnp.logical_not(fwd))
    def _():
        @pl.when(c == n_t - 1)
        def _():
            st_ref[...] = h0b_ref[...]

        def body(i, h):
            o = pl.multiple_of((tt - 1 - i) * SUBLANES, SUBLANES)
            h = a_ref[pl.ds(o, SUBLANES), :] * h + b_ref[pl.ds(o, SUBLANES), :]
            tot = hfall_ref[pl.ds(pl.multiple_of(base + o, SUBLANES), SUBLANES), :] + h
            for s in range(gw // LANES):
                hs_ref[s, pl.ds(o, SUBLANES), :] = tot[:, s * LANES:(s + 1) * LANES]
            return h

        h = lax.fori_loop(0, tt, body, st_ref[...], unroll=8)
        st_ref[...] = h

        @pl.when(c == 0)
        def _():
            hfb_ref[...] = h


def _rnn_scan(xr, conv_w, conv_b, wg, bg, lam, h0f, h0b, t):
    ng, ns, trows, _ = xr.shape
    p = ns * LANES
    nq = p // MXU_DIM
    spg = MXU_DIM // LANES
    tt = min(t, 128)
    n_t = t // tt
    rows = tt * SUBLANES
    halo_l = (CONV_W // 2) * SUBLANES
    halo_r = (CONV_W - 1 - CONV_W // 2) * SUBLANES

    def chunk(k):
        return jnp.where(k < n_t, k, 2 * n_t - 1 - k)

    def out_chunk(k):
        return jnp.where(k < n_t, n_t - 1, 2 * n_t - 1 - k)

    def phase(k):
        return jnp.where(k < n_t, 0, 1)

    return pl.pallas_call(
        functools.partial(_r2_kernel, tt=tt, n_t=n_t),
        grid=(ng, nq, 2 * n_t),
        in_specs=[
            pl.BlockSpec((None, spg, rows, LANES), lambda g, q, k: (g, q, chunk(k), 0)),
            pl.BlockSpec((None, spg, halo_l, LANES),
                         lambda g, q, k: (g, q, jnp.maximum(chunk(k) * (rows // halo_l) - 1, 0), 0)),
            pl.BlockSpec((None, spg, halo_r, LANES),
                         lambda g, q, k: (g, q, jnp.minimum((chunk(k) + 1) * (rows // halo_r), trows // halo_r - 1), 0)),
            pl.BlockSpec((CONV_W, MXU_DIM), lambda g, q, k: (0, q)),
            pl.BlockSpec((1, MXU_DIM), lambda g, q, k: (0, q)),
            pl.BlockSpec((None, None, MXU_DIM, 2 * MXU_DIM), lambda g, q, k: (q, phase(k), 0, 0)),
            pl.BlockSpec((None, None, 1, 2 * MXU_DIM), lambda g, q, k: (q, phase(k), 0, 0)),
            pl.BlockSpec((None, 1, MXU_DIM), lambda g, q, k: (phase(k), 0, q)),
            pl.BlockSpec((None, SUBLANES, MXU_DIM), lambda g, q, k: (g, 0, q)),
            pl.BlockSpec((None, SUBLANES, MXU_DIM), lambda g, q, k: (g, 0, q)),
        ],
        out_specs=[
            pl.BlockSpec((None, spg, rows, LANES), lambda g, q, k: (g, q, out_chunk(k), 0)),
            pl.BlockSpec((None, SUBLANES, MXU_DIM), lambda g, q, k: (g, 0, q)),
            pl.BlockSpec((None, SUBLANES, MXU_DIM), lambda g, q, k: (g, 0, q)),
        ],
        out_shape=[
            jax.ShapeDtypeStruct(xr.shape, F32),
            jax.ShapeDtypeStruct((ng, SUBLANES, p), F32),
            jax.ShapeDtypeStruct((ng, SUBLANES, p), F32),
        ],
        scratch_shapes=[
            pltpu.VMEM((halo_l + rows + halo_r, MXU_DIM), F32),
            pltpu.VMEM((rows, MXU_DIM), F32),
            pltpu.VMEM((rows, MXU_DIM), F32),
            pltpu.VMEM((trows, MXU_DIM), F32),
            pltpu.VMEM((SUBLANES, MXU_DIM), F32),
        ],
        compiler_params=_params("parallel", "parallel", "arbitrary"),
        name="rnn_scan",
    )(xr, xr, xr, conv_w, conv_b, wg, bg, lam, h0f, h0b)


def _gelu_tanh(x):
    return 0.5 * x * (1.0 + jnp.tanh(math.sqrt(2.0 / math.pi) * (x + 0.044715 * (x * x * x))))


def _r3_kernel(x_ref, hs_ref, gate_ref, mod_ref, wo_ref, o_ref, z_ref, *, tt, p):
    for b in range(SUBLANES):
        for j in range(p // LANES):
            hsb = hs_ref[j, pl.ds(b, tt, stride=SUBLANES), :]
            gt = gate_ref[b, :, j * LANES:(j + 1) * LANES]
            z_ref[b * tt:(b + 1) * tt, j * LANES:(j + 1) * LANES] = (hsb * _gelu_tanh(gt)).astype(z_ref.dtype)
    yx = _dot(z_ref[...], wo_ref[...])
    d = yx.shape[-1]
    o_ref[...] = x_ref[...] + mod_ref[:, 2:3, :] * yx.reshape(SUBLANES, tt, d)


def _rnn_out(xa, hs, gate, mod, wout_p):
    bsz, t, d = xa.shape
    p = gate.shape[-1]
    ng = bsz // SUBLANES
    ns = p // LANES
    tt = min(t, 64)
    return pl.pallas_call(
        functools.partial(_r3_kernel, tt=tt, p=p),
        grid=(ng, t // tt),
        in_specs=[
            pl.BlockSpec((SUBLANES, tt, d), lambda gi, i: (gi, i, 0)),
            pl.BlockSpec((None, ns, tt * SUBLANES, LANES), lambda gi, i: (gi, 0, i, 0)),
            pl.BlockSpec((SUBLANES, tt, p), lambda gi, i: (gi, i, 0)),
            pl.BlockSpec((SUBLANES, N_MOD, d), lambda gi, i: (gi, 0, 0)),
            _resident((p, d), lambda gi, i: (0, 0)),
        ],
        out_specs=pl.BlockSpec((SUBLANES, tt, d), lambda gi, i: (gi, i, 0)),
        out_shape=jax.ShapeDtypeStruct(xa.shape, F32),
        scratch_shapes=[pltpu.VMEM((SUBLANES * tt, p), _MXU_DTYPE)],
        compiler_params=_params("parallel", "parallel"),
        name="rnn_out_proj",
    )(xa, hs, gate, mod, wout_p)


def _rnn_weights(lay, w_in, conv_w, conv_b, w_a, b_a, w_i, b_i, lam, w_out):
    d_rnn = conv_b.shape[-1]
    win_p = jnp.concatenate([lay.pad_last(w_in[:, :d_rnn]), lay.pad_last(w_in[:, d_rnn:])], axis=1).astype(_MXU_DTYPE)
    cw_p = lay.pad_last(conv_w)
    cb_p = lay.pad_last(conv_b).reshape(1, lay.p)
    wg = jnp.stack([
        jnp.concatenate([lay.block_diag(w_a[dd]), lay.block_diag(w_i[dd])], axis=-1) for dd in range(2)
    ], axis=1).astype(_MXU_DTYPE)
    bg = jnp.stack([
        jnp.concatenate([lay.pad_last(b_a[dd]).reshape(lay.n_groups, 1, MXU_DIM),
                         lay.pad_last(b_i[dd]).reshape(lay.n_groups, 1, MXU_DIM)], axis=-1) for dd in range(2)
    ], axis=1)
    lam_p = lay.pad_last(lam).reshape(2, 1, lay.p)
    wout_p = lay.pad_rows(w_out).astype(_MXU_DTYPE)
    return win_p, cw_p, cb_p, wg, bg, lam_p, wout_p


def kernel(x, c, ctx, c_ctx, w_mod, b_mod, norm_g, w_fourier, w_rnn_in, conv_w, conv_b,
           w_a, b_a, w_i, b_i, lam, w_rnn_out, w1, b1, w2, b2, final_g):
    bsz, seq, d = x.shape
    depth = w_mod.shape[0]
    assert bsz % SUBLANES == 0 and d % (N_FOURIER_GROUPS * LANES) == 0
    lay = _RnnLayout(w_a.shape[2], w_a.shape[-1])
    ng = bsz // SUBLANES

    pad_rows = (-(bsz + 1)) % SUBLANES
    cv = jnp.concatenate([c, c_ctx[None, :], jnp.zeros((pad_rows, d), F32)], axis=0)
    mod = _modulation(cv, w_mod, b_mod).reshape(depth, bsz + 1 + pad_rows, N_MOD, d)
    mod_x = mod[:, :bsz]
    mod_c = jnp.broadcast_to(mod[:, bsz:bsz + 1], mod_x.shape)

    w1b, w2b = w1.astype(_MXU_DTYPE), w2.astype(_MXU_DTYPE)
    wfb = w_fourier.astype(_MXU_DTYPE)

    for i in range(depth):
        last = i == depth - 1
        j = i // 2
        mx, mc = mod_x[i], mod_c[i]
        if i % 2 == 0:
            x = _fourier_layer(x, mx, norm_g[i, 0], wfb[j])
            if not last:
                ctx = _fourier_layer(ctx, mc, norm_g[i, 0], wfb[j])
        else:
            win_p, cw_p, cb_p, wg, bg, lam_p, wout_p = _rnn_weights(
                lay, w_rnn_in[j], conv_w[j], conv_b[j], w_a[j], b_a[j], w_i[j], b_i[j], lam[j], w_rnn_out[j])
            zeros = jnp.zeros((ng, SUBLANES, lay.p), F32)
            gate_c, xr_c = _rnn_in(ctx, mc, norm_g[i, 0], win_p, lay.p)
            hs_c, hf_c, hb_c = _rnn_scan(xr_c, cw_p, cb_p, wg, bg, lam_p, zeros, zeros, ctx.shape[1])
            gate_x, xr_x = _rnn_in(x, mx, norm_g[i, 0], win_p, lay.p)
            hs_x, _, _ = _rnn_scan(xr_x, cw_p, cb_p, wg, bg, lam_p, hf_c, hb_c, seq)
            x = _rnn_out(x, hs_x, gate_x, mx, wout_p)
            if not last:
                ctx = _rnn_out(ctx, hs_c, gate_c, mc, wout_p)
        x = _mlp(x, mx, norm_g[i, 1], w1b[i], b1[i], w2b[i], b2[i], final_g if last else None)
        if not last:
            ctx = _mlp(ctx, mc, norm_g[i, 1], w1b[i], b1[i], w2b[i], b2[i])
    return x
```

```python
import functools
import math

import numpy as np
import jax
import jax.numpy as jnp
from jax import lax
from jax.experimental import pallas as pl
from jax.experimental.pallas import tpu as pltpu

EPS = 1e-6
RG_C = 8.0
N_FOURIER_GROUPS = 4
CONV_W = 4
N_MOD = 6

LANES = 128
SUBLANES = 8
MXU_DIM = 256
VMEM_LIMIT = 56 << 20
_MXU_DTYPE = jnp.bfloat16
_STORE_DTYPE = jnp.bfloat16
F32 = jnp.float32


def _params(*sem):
    return pltpu.CompilerParams(dimension_semantics=sem, vmem_limit_bytes=VMEM_LIMIT)


def _resident(shape, index_map):
    return pl.BlockSpec(shape, index_map, pipeline_mode=pl.Buffered(1))


def _dot(a, b):
    return jnp.dot(a, b, preferred_element_type=F32)


def _norm_mod(x, g, shift, scale):
    ms = jnp.mean(x * x, axis=-1, keepdims=True)
    y = x * lax.rsqrt(ms + EPS) * g
    return y * (1.0 + scale) + shift


def _mod_kernel(cv_ref, w_ref, b_ref, o_ref):
    c = cv_ref[...]
    s = c * jax.nn.sigmoid(c)
    o_ref[...] = _dot(s.astype(_MXU_DTYPE), w_ref[...].astype(_MXU_DTYPE)) + b_ref[...]


def _modulation(cv, w_mod, b_mod):
    depth, d, n = w_mod.shape
    rows = cv.shape[0]
    tn = min(n, 1024)
    return pl.pallas_call(
        _mod_kernel,
        grid=(depth, n // tn),
        in_specs=[
            pl.BlockSpec((rows, d), lambda i, j: (0, 0)),
            pl.BlockSpec((None, d, tn), lambda i, j: (i, 0, j)),
            pl.BlockSpec((None, 1, tn), lambda i, j: (i, 0, j)),
        ],
        out_specs=pl.BlockSpec((None, rows, tn), lambda i, j: (i, 0, j)),
        out_shape=jax.ShapeDtypeStruct((depth, rows, n), F32),
        compiler_params=_params("parallel", "parallel"),
        name="modulation",
    )(cv, w_mod, b_mod.reshape(depth, 1, n))


def _mlp_kernel(x_ref, mod_ref, g_ref, w1_ref, b1_ref, w2_ref, b2_ref, *rest, tf, final):
    if final:
        fg_ref, o_ref = rest
    else:
        (o_ref,) = rest
    x = x_ref[...]
    h = _norm_mod(x, g_ref[...], mod_ref[3:4, :], mod_ref[4:5, :]).astype(_MXU_DTYPE)
    dff = w1_ref.shape[1]
    acc = jnp.zeros(x.shape, F32)
    for j in range(dff // tf):
        a = _dot(h, w1_ref[:, j * tf:(j + 1) * tf]) + b1_ref[:, j * tf:(j + 1) * tf]
        a = jnp.maximum(a, 0.0)
        a = a * a
        acc = acc + _dot(a.astype(_MXU_DTYPE), w2_ref[j * tf:(j + 1) * tf, :])
    out = x + mod_ref[5:6, :] * (acc + b2_ref[...])
    if final:
        ms = jnp.mean(out * out, axis=-1, keepdims=True)
        out = out * lax.rsqrt(ms + EPS) * fg_ref[...]
    o_ref[...] = out


def _mlp(xa, mod, g, w1, b1, w2, b2, final_g=None):
    bsz, t, d = xa.shape
    dff = w1.shape[1]
    tm = min(t, 512)
    tf = min(dff, 1024)
    final = final_g is not None
    in_specs = [
        pl.BlockSpec((None, tm, d), lambda b, i: (b, i, 0)),
        pl.BlockSpec((None, N_MOD, d), lambda b, i: (b, 0, 0)),
        pl.BlockSpec((1, d), lambda b, i: (0, 0)),
        _resident((d, dff), lambda b, i: (0, 0)),
        pl.BlockSpec((1, dff), lambda b, i: (0, 0)),
        _resident((dff, d), lambda b, i: (0, 0)),
        pl.BlockSpec((1, d), lambda b, i: (0, 0)),
    ]
    args = [xa, mod, g.reshape(1, d), w1, b1.reshape(1, dff), w2, b2.reshape(1, d)]
    if final:
        in_specs.append(pl.BlockSpec((1, d), lambda b, i: (0, 0)))
        args.append(final_g.reshape(1, d))
    return pl.pallas_call(
        functools.partial(_mlp_kernel, tf=tf, final=final),
        grid=(bsz, t // tm),
        in_specs=in_specs,
        out_specs=pl.BlockSpec((None, tm, d), lambda b, i: (b, i, 0)),
        out_shape=jax.ShapeDtypeStruct(xa.shape, F32),
        compiler_params=_params("parallel", "parallel"),
        name="mlp",
    )(*args)


def _dft_tables(t, gs):
    c = np.arange(gs, dtype=np.int64)
    ang = 2.0 * np.pi * ((c[:, None] * c[None, :]) % gs) / gs
    wg = np.concatenate([np.cos(ang), -np.sin(ang)], axis=1)
    k = np.arange(t, dtype=np.int64)
    angt = 2.0 * np.pi * ((k[:, None] * k[None, :]) % t) / t
    cs = np.stack([np.cos(angt), np.sin(angt)])
    return jnp.asarray(wg, _MXU_DTYPE), jnp.asarray(cs, _MXU_DTYPE)


def _f1_kernel(x_ref, mod_ref, g_ref, wg_ref, o_ref, *, gs):
    h = _norm_mod(x_ref[...], g_ref[...], mod_ref[0:1, :], mod_ref[1:2, :]).astype(_MXU_DTYPE)
    for q in range(h.shape[-1] // gs):
        res = _dot(h[:, q * gs:(q + 1) * gs], wg_ref[...])
        o_ref[0, :, q * gs:(q + 1) * gs] = res[:, :gs].astype(o_ref.dtype)
        o_ref[1, :, q * gs:(q + 1) * gs] = res[:, gs:].astype(o_ref.dtype)


def _f2_kernel(x_ref, g2_ref, cs_ref, mod_ref, wo_ref, o_ref, *, tm, scale):
    r0 = pl.multiple_of(pl.program_id(1) * tm, tm)
    y = _dot(cs_ref[0, pl.ds(r0, tm), :], g2_ref[0]) + _dot(cs_ref[1, pl.ds(r0, tm), :], g2_ref[1])
    yx = _dot((y * scale).astype(_MXU_DTYPE), wo_ref[...])
    o_ref[...] = x_ref[...] + mod_ref[2:3, :] * yx


def _fourier_layer(xa, mod, g, w_out):
    bsz, t, d = xa.shape
    gs = d // N_FOURIER_GROUPS
    tm = min(t, 512)
    wg, cs = _dft_tables(t, gs)
    g2 = pl.pallas_call(
        functools.partial(_f1_kernel, gs=gs),
        grid=(bsz, t // tm),
        in_specs=[
            pl.BlockSpec((None, tm, d), lambda b, i: (b, i, 0)),
            pl.BlockSpec((None, N_MOD, d), lambda b, i: (b, 0, 0)),
            pl.BlockSpec((1, d), lambda b, i: (0, 0)),
            pl.BlockSpec((gs, 2 * gs), lambda b, i: (0, 0)),
        ],
        out_specs=pl.BlockSpec((None, 2, tm, d), lambda b, i: (b, 0, i, 0)),
        out_shape=jax.ShapeDtypeStruct((bsz, 2, t, d), _MXU_DTYPE),
        compiler_params=_params("parallel", "parallel"),
        name="fourier_group_dft",
    )(xa, mod, g.reshape(1, d), wg)
    scale = 1.0 / math.sqrt(t * gs)
    return pl.pallas_call(
        functools.partial(_f2_kernel, tm=tm, scale=scale),
        grid=(bsz, t // tm),
        in_specs=[
            pl.BlockSpec((None, tm, d), lambda b, i: (b, i, 0)),
            pl.BlockSpec((None, 2, t, d), lambda b, i: (b, 0, 0, 0)),
            _resident((2, t, t), lambda b, i: (0, 0, 0)),
            pl.BlockSpec((None, N_MOD, d), lambda b, i: (b, 0, 0)),
            _resident((d, d), lambda b, i: (0, 0)),
        ],
        out_specs=pl.BlockSpec((None, tm, d), lambda b, i: (b, i, 0)),
        out_shape=jax.ShapeDtypeStruct(xa.shape, F32),
        compiler_params=_params("parallel", "arbitrary"),
        name="fourier_time_dft",
    )(xa, g2, cs, mod, w_out)


class _RnnLayout:
    def __init__(self, n_blocks, rb):
        self.rb = rb
        self.n_blocks = n_blocks
        self.per_group = MXU_DIM // rb
        self.n_groups = -(-n_blocks // self.per_group)
        self.p = self.n_groups * MXU_DIM

    def pad_last(self, v):
        pg, rb = self.per_group, self.rb
        lead = v.shape[:-1]
        nl = len(lead)
        vb = v.reshape(lead + (self.n_blocks, rb))
        vb = jnp.pad(vb, [(0, 0)] * nl + [(0, self.n_groups * pg - self.n_blocks), (0, 0)])
        vg = vb.reshape(lead + (self.n_groups, pg * rb))
        vg = jnp.pad(vg, [(0, 0)] * nl + [(0, 0), (0, MXU_DIM - pg * rb)])
        return vg.reshape(lead + (self.p,))

    def pad_rows(self, w):
        return self.pad_last(w.T).T

    def block_diag(self, w):
        pg, rb = self.per_group, self.rb
        tot = self.n_groups * pg
        wp = jnp.pad(w, ((0, tot - self.n_blocks), (0, 0), (0, 0))).reshape(self.n_groups, pg, rb, 1, rb)
        eye = jnp.eye(pg, dtype=w.dtype).reshape(1, pg, 1, pg, 1)
        m = (wp * eye).reshape(self.n_groups, pg * rb, pg * rb)
        extra = MXU_DIM - pg * rb
        return jnp.pad(m, ((0, 0), (0, extra), (0, extra)))


def _r1_kernel(x_ref, mod_ref, g_ref, win_ref, gate_ref, xr_ref, *, tt, p):
    x = x_ref[...]
    m = mod_ref[...]
    h = _norm_mod(x, g_ref[...], m[:, 0:1, :], m[:, 1:2, :])
    hb = h.reshape(SUBLANES * tt, h.shape[-1]).astype(_MXU_DTYPE)
    u = _dot(hb, win_ref[...])
    gate_ref[...] = u[:, :p].reshape(SUBLANES, tt, p).astype(gate_ref.dtype)
    for b in range(SUBLANES):
        for j in range(p // LANES):
            xr_ref[j, pl.ds(b, tt, stride=SUBLANES), :] = u[b * tt:(b + 1) * tt, p + j * LANES:p + (j + 1) * LANES]


def _rnn_in(xa, mod, g, win_p, p):
    bsz, t, d = xa.shape
    ng = bsz // SUBLANES
    tt = min(t, 64)
    ns = p // LANES
    return pl.pallas_call(
        functools.partial(_r1_kernel, tt=tt, p=p),
        grid=(ng, t // tt),
        in_specs=[
            pl.BlockSpec((SUBLANES, tt, d), lambda gi, i: (gi, i, 0)),
            pl.BlockSpec((SUBLANES, N_MOD, d), lambda gi, i: (gi, 0, 0)),
            pl.BlockSpec((1, d), lambda gi, i: (0, 0)),
            _resident((d, 2 * p), lambda gi, i: (0, 0)),
        ],
        out_specs=[
            pl.BlockSpec((SUBLANES, tt, p), lambda gi, i: (gi, i, 0)),
            pl.BlockSpec((None, ns, tt * SUBLANES, LANES), lambda gi, i: (gi, 0, i, 0)),
        ],
        out_shape=[
            jax.ShapeDtypeStruct((bsz, t, p), _STORE_DTYPE),
            jax.ShapeDtypeStruct((ng, ns, t * SUBLANES, LANES), F32),
        ],
        compiler_params=_params("parallel", "parallel"),
        name="rnn_in_proj",
    )(xa, mod, g.reshape(1, d), win_p)


def _softplus(z):
    return jnp.maximum(z, 0.0) + jnp.log(1.0 + jnp.exp(-jnp.abs(z)))


def _pack_bf16_pair(hi, lo):
    hb = lax.bitcast_convert_type(hi.astype(jnp.bfloat16).astype(F32), jnp.uint32)
    lb = lax.bitcast_convert_type(lo.astype(jnp.bfloat16).astype(F32), jnp.uint32)
    return hb | (lb >> 16)


def _unpack_bf16_pair(w):
    hi = lax.bitcast_convert_type(w & jnp.uint32(0xFFFF0000), F32)
    lo = lax.bitcast_convert_type(w << 16, F32)
    return hi, lo


def _r2_kernel(xm_ref, xp_ref, xn_ref, cw_ref, cb_ref, wg_ref, bg_ref, lam_ref, h0f_ref, h0b_ref,
               hs_ref, hff_ref, hfb_ref, ext_ref, a_ref, b_ref, hfall_ref, st_ref, *, tt, n_t):
    k = pl.program_id(2)
    rows = tt * SUBLANES
    halo_l = (CONV_W // 2) * SUBLANES
    halo_r = (CONV_W - 1 - CONV_W // 2) * SUBLANES
    gw = MXU_DIM

    def compute():
        c = jnp.where(k < n_t, k, 2 * n_t - 1 - k)
        for s in range(gw // LANES):
            ls = slice(s * LANES, (s + 1) * LANES)
            ext_ref[0:halo_l, ls] = jnp.where(c > 0, xp_ref[s], 0.0)
            ext_ref[halo_l:halo_l + rows, ls] = xm_ref[s]
            ext_ref[halo_l + rows:halo_l + rows + halo_r, ls] = jnp.where(c < n_t - 1, xn_ref[s], 0.0)
        xh = cb_ref[...] + sum(
            cw_ref[kk:kk + 1, :] * ext_ref[kk * SUBLANES:kk * SUBLANES + rows, :] for kk in range(CONV_W))
        p = _dot(xh.astype(_MXU_DTYPE), wg_ref[...]) + bg_ref[...]
        th_r = jnp.tanh(p[:, :gw])
        th_i = jnp.tanh(p[:, gw:])
        kh = (-0.5 * RG_C * math.log2(math.e)) * _softplus(-lam_ref[...])
        a = jnp.exp2(kh * th_r + kh)
        y = 1.0 - a * a
        a_ref[...] = a
        b_ref[...] = (y * lax.rsqrt(jnp.maximum(y, 1e-30))) * ((th_i + 1.0) * xh)

    def scan(fwd):
        j = k - 1
        if fwd:
            base = j * rows
            h = jnp.where(j == 0, h0f_ref[...], st_ref[...])
        else:
            base = (2 * n_t - 1 - j) * rows
            h = jnp.where(j == n_t, h0b_ref[...], st_ref[...])
        for t in (range(tt) if fwd else reversed(range(tt))):
            o = t * SUBLANES
            h = a_ref[o:o + SUBLANES, :] * h + b_ref[o:o + SUBLANES, :]
            hrow = pl.ds(pl.multiple_of(base + o, SUBLANES), SUBLANES)
            if fwd:
                hfall_ref[hrow, :] = h
            else:
                tot = hfall_ref[hrow, :] + h
                hs_ref[o:o + SUBLANES, :] = _pack_bf16_pair(tot[:, :LANES], tot[:, LANES:])
        st_ref[...] = h

    @pl.when(k == 0)
    def _():
        compute()

    @pl.when(jnp.logical_and(k >= 1, k <= n_t))
    def _():
        scan(True)
        compute()

    @pl.when(k == n_t)
    def _():
        hff_ref[...] = st_ref[...]

    @pl.when(jnp.logical_and(k > n_t, k < 2 * n_t))
    def _():
        scan(False)
        compute()

    @pl.when(k == 2 * n_t)
    def _():
        scan(False)
        hfb_ref[...] = st_ref[...]


def _rnn_scan(xr, conv_w, conv_b, wg, bg, lam, h0f, h0b, t):
    ng, ns, trows, _ = xr.shape
    p = ns * LANES
    nq = p // MXU_DIM
    spg = MXU_DIM // LANES
    assert spg == 2
    tt = min(t, 128)
    n_t = t // tt
    rows = tt * SUBLANES
    halo_l = (CONV_W // 2) * SUBLANES
    halo_r = (CONV_W - 1 - CONV_W // 2) * SUBLANES

    def chunk(k):
        kk = jnp.minimum(k, 2 * n_t - 1)
        return jnp.where(kk < n_t, kk, 2 * n_t - 1 - kk)

    def out_chunk(k):
        return jnp.where(k <= n_t, n_t - 1, 2 * n_t - k)

    def phase(k):
        return jnp.where(k < n_t, 0, 1)

    return pl.pallas_call(
        functools.partial(_r2_kernel, tt=tt, n_t=n_t),
        grid=(ng, nq, 2 * n_t + 1),
        in_specs=[
            pl.BlockSpec((None, spg, rows, LANES), lambda g, q, k: (g, q, chunk(k), 0)),
            pl.BlockSpec((None, spg, halo_l, LANES),
                         lambda g, q, k: (g, q, jnp.maximum(chunk(k) * (rows // halo_l) - 1, 0), 0)),
            pl.BlockSpec((None, spg, halo_r, LANES),
                         lambda g, q, k: (g, q, jnp.minimum((chunk(k) + 1) * (rows // halo_r), trows // halo_r - 1), 0)),
            pl.BlockSpec((CONV_W, MXU_DIM), lambda g, q, k: (0, q)),
            pl.BlockSpec((1, MXU_DIM), lambda g, q, k: (0, q)),
            pl.BlockSpec((None, None, MXU_DIM, 2 * MXU_DIM), lambda g, q, k: (q, phase(k), 0, 0)),
            pl.BlockSpec((None, None, 1, 2 * MXU_DIM), lambda g, q, k: (q, phase(k), 0, 0)),
            pl.BlockSpec((None, 1, MXU_DIM), lambda g, q, k: (phase(k), 0, q)),
            pl.BlockSpec((None, SUBLANES, MXU_DIM), lambda g, q, k: (g, 0, q)),
            pl.BlockSpec((None, SUBLANES, MXU_DIM), lambda g, q, k: (g, 0, q)),
        ],
        out_specs=[
            pl.BlockSpec((None, None, rows, LANES), lambda g, q, k: (g, q, out_chunk(k), 0)),
            pl.BlockSpec((None, SUBLANES, MXU_DIM), lambda g, q, k: (g, 0, q)),
            pl.BlockSpec((None, SUBLANES, MXU_DIM), lambda g, q, k: (g, 0, q)),
        ],
        out_shape=[
            jax.ShapeDtypeStruct((ng, nq, trows, LANES), jnp.uint32),
            jax.ShapeDtypeStruct((ng, SUBLANES, p), F32),
            jax.ShapeDtypeStruct((ng, SUBLANES, p), F32),
        ],
        scratch_shapes=[
            pltpu.VMEM((halo_l + rows + halo_r, MXU_DIM), F32),
            pltpu.VMEM((rows, MXU_DIM), F32),
            pltpu.VMEM((rows, MXU_DIM), F32),
            pltpu.VMEM((trows, MXU_DIM), F32),
            pltpu.VMEM((SUBLANES, MXU_DIM), F32),
        ],
        compiler_params=_params("parallel", "parallel", "arbitrary"),
        name="rnn_scan",
    )(xr, xr, xr, conv_w, conv_b, wg, bg, lam, h0f, h0b)


def _gelu_tanh(x):
    return 0.5 * x * (1.0 + jnp.tanh(math.sqrt(2.0 / math.pi) * (x + 0.044715 * (x * x * x))))


def _r3_kernel(x_ref, hs_ref, gate_ref, mod_ref, wo_ref, o_ref, z_ref, *, tt, p):
    for b in range(SUBLANES):
        for q in range(p // MXU_DIM):
            halves = _unpack_bf16_pair(hs_ref[q, pl.ds(b, tt, stride=SUBLANES), :])
            for s, hv in enumerate(halves):
                cols = slice(q * MXU_DIM + s * LANES, q * MXU_DIM + (s + 1) * LANES)
                gt = gate_ref[b, :, cols].astype(F32)
                z_ref[b * tt:(b + 1) * tt, cols] = (hv * _gelu_tanh(gt)).astype(z_ref.dtype)
    yx = _dot(z_ref[...], wo_ref[...])
    d = yx.shape[-1]
    o_ref[...] = x_ref[...] + mod_ref[:, 2:3, :] * yx.reshape(SUBLANES, tt, d)


def _rnn_out(xa, hs, gate, mod, wout_p):
    bsz, t, d = xa.shape
    p = gate.shape[-1]
    ng = bsz // SUBLANES
    nq = p // MXU_DIM
    tt = min(t, 64)
    return pl.pallas_call(
        functools.partial(_r3_kernel, tt=tt, p=p),
        grid=(ng, t // tt),
        in_specs=[
            pl.BlockSpec((SUBLANES, tt, d), lambda gi, i: (gi, i, 0)),
            pl.BlockSpec((None, nq, tt * SUBLANES, LANES), lambda gi, i: (gi, 0, i, 0)),
            pl.BlockSpec((SUBLANES, tt, p), lambda gi, i: (gi, i, 0)),
            pl.BlockSpec((SUBLANES, N_MOD, d), lambda gi, i: (gi, 0, 0)),
            _resident((p, d), lambda gi, i: (0, 0)),
        ],
        out_specs=pl.BlockSpec((SUBLANES, tt, d), lambda gi, i: (gi, i, 0)),
        out_shape=jax.ShapeDtypeStruct(xa.shape, F32),
        scratch_shapes=[pltpu.VMEM((SUBLANES * tt, p), _MXU_DTYPE)],
        compiler_params=_params("parallel", "parallel"),
        name="rnn_out_proj",
    )(xa, hs, gate, mod, wout_p)


def _rnn_weights(lay, w_in, conv_w, conv_b, w_a, b_a, w_i, b_i, lam, w_out):
    d_rnn = conv_b.shape[-1]
    win_p = jnp.concatenate([lay.pad_last(w_in[:, :d_rnn]), lay.pad_last(w_in[:, d_rnn:])], axis=1).astype(_MXU_DTYPE)
    cw_p = 0.5 * lay.pad_last(conv_w)
    cb_p = 0.5 * lay.pad_last(conv_b).reshape(1, lay.p)
    wg = jnp.stack([
        jnp.concatenate([lay.block_diag(w_a[dd]), lay.block_diag(w_i[dd])], axis=-1) for dd in range(2)
    ], axis=1).astype(_MXU_DTYPE)
    bg = 0.5 * jnp.stack([
        jnp.concatenate([lay.pad_last(b_a[dd]).reshape(lay.n_groups, 1, MXU_DIM),
                         lay.pad_last(b_i[dd]).reshape(lay.n_groups, 1, MXU_DIM)], axis=-1) for dd in range(2)
    ], axis=1)
    lam_p = lay.pad_last(lam).reshape(2, 1, lay.p)
    wout_p = lay.pad_rows(w_out).astype(_MXU_DTYPE)
    return win_p, cw_p, cb_p, wg, bg, lam_p, wout_p


def kernel(x, c, ctx, c_ctx, w_mod, b_mod, norm_g, w_fourier, w_rnn_in, conv_w, conv_b,
           w_a, b_a, w_i, b_i, lam, w_rnn_out, w1, b1, w2, b2, final_g):
    bsz, seq, d = x.shape
    depth = w_mod.shape[0]
    assert bsz % SUBLANES == 0 and d % (N_FOURIER_GROUPS * LANES) == 0
    lay = _RnnLayout(w_a.shape[2], w_a.shape[-1])
    ng = bsz // SUBLANES

    pad_rows = (-(bsz + 1)) % SUBLANES
    cv = jnp.concatenate([c, c_ctx[None, :], jnp.zeros((pad_rows, d), F32)], axis=0)
    mod = _modulation(cv, w_mod, b_mod).reshape(depth, bsz + 1 + pad_rows, N_MOD, d)
    mod_x = mod[:, :bsz]
    mod_c = jnp.broadcast_to(mod[:, bsz:bsz + 1], mod_x.shape)

    w1b, w2b = w1.astype(_MXU_DTYPE), w2.astype(_MXU_DTYPE)
    wfb = w_fourier.astype(_MXU_DTYPE)

    for i in range(depth):
        last = i == depth - 1
        j = i // 2
        mx, mc = mod_x[i], mod_c[i]
        if i % 2 == 0:
            x = _fourier_layer(x, mx, norm_g[i, 0], wfb[j])
            if not last:
                ctx = _fourier_layer(ctx, mc, norm_g[i, 0], wfb[j])
        else:
            win_p, cw_p, cb_p, wg, bg, lam_p, wout_p = _rnn_weights(
                lay, w_rnn_in[j], conv_w[j], conv_b[j], w_a[j], b_a[j], w_i[j], b_i[j], lam[j], w_rnn_out[j])
            zeros = jnp.zeros((ng, SUBLANES, lay.p), F32)
            gate_c, xr_c = _rnn_in(ctx, mc, norm_g[i, 0], win_p, lay.p)
            hs_c, hf_c, hb_c = _rnn_scan(xr_c, cw_p, cb_p, wg, bg, lam_p, zeros, zeros, ctx.shape[1])
            gate_x, xr_x = _rnn_in(x, mx, norm_g[i, 0], win_p, lay.p)
            hs_x, _, _ = _rnn_scan(xr_x, cw_p, cb_p, wg, bg, lam_p, hf_c, hb_c, seq)
            x = _rnn_out(x, hs_x, gate_x, mx, wout_p)
            if not last:
                ctx = _rnn_out(ctx, hs_c, gate_c, mc, wout_p)
        x = _mlp(x, mx, norm_g[i, 1], w1b[i], b1[i], w2b[i], b2[i], final_g if last else None)
        if not last:
            ctx = _mlp(ctx, mc, norm_g[i, 1], w1b[i], b1[i], w2b[i], b2[i])
    return x
```

```python
import functools
import math

import numpy as np
import jax
import jax.numpy as jnp
from jax import lax
from jax.experimental import pallas as pl
from jax.experimental.pallas import tpu as pltpu

EPS = 1e-6
RG_C = 8.0
N_FOURIER_GROUPS = 4
CONV_W = 4
N_MOD = 6

LANES = 128
SUBLANES = 8
MXU_DIM = 256
VMEM_LIMIT = 56 << 20
_MXU_DTYPE = jnp.bfloat16
_STORE_DTYPE = jnp.bfloat16
F32 = jnp.float32


def _params(*sem):
    return pltpu.CompilerParams(dimension_semantics=sem, vmem_limit_bytes=VMEM_LIMIT)


def _resident(shape, index_map):
    return pl.BlockSpec(shape, index_map, pipeline_mode=pl.Buffered(1))


def _dot(a, b):
    return jnp.dot(a, b, preferred_element_type=F32)


def _layer_spec(block, layer, index_map, resident=False):
    make = _resident if resident else pl.BlockSpec
    return make((None,) + tuple(block), lambda *ids: (layer,) + tuple(index_map(*ids)))


class _Mod:
    def __init__(self, table, table_ctx8, layer, is_ctx, bsz):
        self.table, self.table_ctx8, self.layer, self.is_ctx, self.bsz = table, table_ctx8, layer, is_ctx, bsz

    def per_batch(self):
        d = self.table.shape[-1]
        layer, fixed = self.layer, (self.bsz if self.is_ctx else None)
        return self.table, pl.BlockSpec(
            (None, None, N_MOD, d), lambda b, i: (layer, b if fixed is None else fixed, 0, 0))

    def per_group(self):
        d = self.table.shape[-1]
        layer = self.layer
        if self.is_ctx:
            return self.table_ctx8, pl.BlockSpec((None, SUBLANES, N_MOD, d), lambda gi, i: (layer, 0, 0, 0))
        return self.table, pl.BlockSpec((None, SUBLANES, N_MOD, d), lambda gi, i: (layer, gi, 0, 0))


def _norm_mod(x, g, shift, scale):
    ms = jnp.mean(x * x, axis=-1, keepdims=True)
    y = x * lax.rsqrt(ms + EPS) * g
    return y * (1.0 + scale) + shift


def _mod_kernel(cv_ref, w_ref, b_ref, o_ref):
    c = cv_ref[...]
    s = c * jax.nn.sigmoid(c)
    o_ref[...] = _dot(s.astype(_MXU_DTYPE), w_ref[...].astype(_MXU_DTYPE)) + b_ref[...]


def _modulation(cv, w_mod, b_mod):
    depth, d, n = w_mod.shape
    rows = cv.shape[0]
    tn = min(n, 1024)
    return pl.pallas_call(
        _mod_kernel,
        grid=(depth, n // tn),
        in_specs=[
            pl.BlockSpec((rows, d), lambda i, j: (0, 0)),
            pl.BlockSpec((None, d, tn), lambda i, j: (i, 0, j)),
            pl.BlockSpec((None, 1, tn), lambda i, j: (i, 0, j)),
        ],
        out_specs=pl.BlockSpec((None, rows, tn), lambda i, j: (i, 0, j)),
        out_shape=jax.ShapeDtypeStruct((depth, rows, n), F32),
        compiler_params=_params("parallel", "parallel"),
        name="modulation",
    )(cv, w_mod, b_mod.reshape(depth, 1, n))


def _mlp_kernel(x_ref, mod_ref, g_ref, w1_ref, b1_ref, w2_ref, b2_ref, *rest, tf, final):
    if final:
        fg_ref, o_ref = rest
    else:
        (o_ref,) = rest
    x = x_ref[...]
    h = _norm_mod(x, g_ref[...], mod_ref[3:4, :], mod_ref[4:5, :]).astype(_MXU_DTYPE)
    dff = w1_ref.shape[1]
    acc = jnp.zeros(x.shape, F32)
    for j in range(dff // tf):
        a = _dot(h, w1_ref[:, j * tf:(j + 1) * tf]) + b1_ref[:, j * tf:(j + 1) * tf]
        a = jnp.maximum(a, 0.0)
        a = a * a
        acc = acc + _dot(a.astype(_MXU_DTYPE), w2_ref[j * tf:(j + 1) * tf, :])
    out = x + mod_ref[5:6, :] * (acc + b2_ref[...])
    if final:
        ms = jnp.mean(out * out, axis=-1, keepdims=True)
        out = out * lax.rsqrt(ms + EPS) * fg_ref[...]
    o_ref[...] = out


def _mlp(xa, mod, layer, norm_g, w1, b1, w2, b2, final_g=None):
    bsz, t, d = xa.shape
    dff = w1.shape[-1]
    tm = min(t, 512)
    tf = min(dff, 1024)
    final = final_g is not None
    mod_arr, mod_spec = mod.per_batch()
    zero2 = lambda b, i: (0, 0)
    in_specs = [
        pl.BlockSpec((None, tm, d), lambda b, i: (b, i, 0)),
        mod_spec,
        _layer_spec((1, d), 2 * layer + 1, zero2),
        _layer_spec((d, dff), layer, zero2, resident=True),
        _layer_spec((1, dff), layer, zero2),
        _layer_spec((dff, d), layer, zero2, resident=True),
        _layer_spec((1, d), layer, zero2),
    ]
    args = [xa, mod_arr, norm_g, w1, b1, w2, b2]
    if final:
        in_specs.append(pl.BlockSpec((1, d), zero2))
        args.append(final_g.reshape(1, d))
    return pl.pallas_call(
        functools.partial(_mlp_kernel, tf=tf, final=final),
        grid=(bsz, t // tm),
        in_specs=in_specs,
        out_specs=pl.BlockSpec((None, tm, d), lambda b, i: (b, i, 0)),
        out_shape=jax.ShapeDtypeStruct(xa.shape, F32),
        compiler_params=_params("parallel", "parallel"),
        name="mlp",
    )(*args)


def _dft_tables(t, gs):
    c = np.arange(gs, dtype=np.int64)
    ang = 2.0 * np.pi * ((c[:, None] * c[None, :]) % gs) / gs
    wg = np.concatenate([np.cos(ang), -np.sin(ang)], axis=1)
    k = np.arange(t, dtype=np.int64)
    angt = 2.0 * np.pi * ((k[:, None] * k[None, :]) % t) / t
    cs = np.stack([np.cos(angt), np.sin(angt)])
    return jnp.asarray(wg, _MXU_DTYPE), jnp.asarray(cs, _MXU_DTYPE)


def _f1_kernel(x_ref, mod_ref, g_ref, wg_ref, o_ref, *, gs):
    h = _norm_mod(x_ref[...], g_ref[...], mod_ref[0:1, :], mod_ref[1:2, :]).astype(_MXU_DTYPE)
    for q in range(h.shape[-1] // gs):
        res = _dot(h[:, q * gs:(q + 1) * gs], wg_ref[...])
        o_ref[0, :, q * gs:(q + 1) * gs] = res[:, :gs].astype(o_ref.dtype)
        o_ref[1, :, q * gs:(q + 1) * gs] = res[:, gs:].astype(o_ref.dtype)


def _f2_kernel(x_ref, g2_ref, cs_ref, mod_ref, wo_ref, o_ref, *, tm, scale):
    r0 = pl.multiple_of(pl.program_id(1) * tm, tm)
    y = _dot(cs_ref[0, pl.ds(r0, tm), :], g2_ref[0]) + _dot(cs_ref[1, pl.ds(r0, tm), :], g2_ref[1])
    yx = _dot((y * scale).astype(_MXU_DTYPE), wo_ref[...])
    o_ref[...] = x_ref[...] + mod_ref[2:3, :] * yx


def _fourier_layer(xa, mod, layer, norm_g, w_out):
    bsz, t, d = xa.shape
    mod_arr, mod_spec = mod.per_batch()
    zero2 = lambda b, i: (0, 0)
    jf = layer // 2
    gs = d // N_FOURIER_GROUPS
    tm = min(t, 512)
    wg, cs = _dft_tables(t, gs)
    g2 = pl.pallas_call(
        functools.partial(_f1_kernel, gs=gs),
        grid=(bsz, t // tm),
        in_specs=[
            pl.BlockSpec((None, tm, d), lambda b, i: (b, i, 0)),
            mod_spec,
            _layer_spec((1, d), 2 * layer, zero2),
            pl.BlockSpec((gs, 2 * gs), zero2),
        ],
        out_specs=pl.BlockSpec((None, 2, tm, d), lambda b, i: (b, 0, i, 0)),
        out_shape=jax.ShapeDtypeStruct((bsz, 2, t, d), _MXU_DTYPE),
        compiler_params=_params("parallel", "parallel"),
        name="fourier_group_dft",
    )(xa, mod_arr, norm_g, wg)
    scale = 1.0 / math.sqrt(t * gs)
    return pl.pallas_call(
        functools.partial(_f2_kernel, tm=tm, scale=scale),
        grid=(bsz, t // tm),
        in_specs=[
            pl.BlockSpec((None, tm, d), lambda b, i: (b, i, 0)),
            pl.BlockSpec((None, 2, t, d), lambda b, i: (b, 0, 0, 0)),
            _resident((2, t, t), lambda b, i: (0, 0, 0)),
            mod_spec,
            _layer_spec((d, d), jf, zero2, resident=True),
        ],
        out_specs=pl.BlockSpec((None, tm, d), lambda b, i: (b, i, 0)),
        out_shape=jax.ShapeDtypeStruct(xa.shape, F32),
        compiler_params=_params("parallel", "arbitrary"),
        name="fourier_time_dft",
    )(xa, g2, cs, mod_arr, w_out)


FFT_RADIX = 8
FFT_MIN_BLOCK = MXU_DIM


def _fft_tables(t, gs, tmn):
    n2 = t // FFT_RADIX
    k2 = np.arange(n2, dtype=np.int64)
    tabs = []
    for k1 in range(FFT_RADIX):
        k = k1 + FFT_RADIX * k2
        ang = 2.0 * np.pi * ((k[:, None] * k2[None, :]) % t) / t
        tabs.append(np.concatenate([np.cos(ang), np.sin(ang)], axis=1))
    c = np.arange(gs, dtype=np.int64)
    angc = 2.0 * np.pi * ((c[:, None] * c[None, :]) % gs) / gs
    wg = np.concatenate([np.cos(angc), -np.sin(angc)], axis=1)
    runs = tmn // FFT_RADIX
    pm = np.zeros((tmn, tmn))
    m = np.arange(runs)
    for k1 in range(FFT_RADIX):
        pm[FFT_RADIX * m + k1, k1 * runs + m] = 1.0
    return (jnp.asarray(np.stack(tabs), _MXU_DTYPE), jnp.asarray(wg, _MXU_DTYPE), jnp.asarray(pm, _MXU_DTYPE))


def _dft4(v):
    (ar, ai), (br, bi), (cr, ci), (dr, di) = v
    e0r, e0i, e1r, e1i = ar + cr, ai + ci, ar - cr, ai - ci
    f0r, f0i, f1r, f1i = br + dr, bi + di, br - dr, bi - di
    return [(e0r + f0r, e0i + f0i), (e1r + f1i, e1i - f1r), (e0r - f0r, e0i - f0i), (e1r - f1i, e1i + f1r)]


def _radix8(w):
    s = [(w[i][0] + w[i + 4][0], w[i][1] + w[i + 4][1]) for i in range(4)]
    dd = [(w[i][0] - w[i + 4][0], w[i][1] - w[i + 4][1]) for i in range(4)]
    rt = math.sqrt(0.5)
    tw = [dd[0],
          ((dd[1][0] + dd[1][1]) * rt, (dd[1][1] - dd[1][0]) * rt),
          (dd[2][1], -dd[2][0]),
          ((dd[3][1] - dd[3][0]) * rt, -(dd[3][0] + dd[3][1]) * rt)]
    even, odd = _dft4(s), _dft4(tw)
    return [even[0], odd[0], even[1], odd[1], even[2], odd[2], even[3], odd[3]]


def _fft_kernel(x_ref, mod_ref, g_ref, tab_ref, wg_ref, pm_ref, wo_ref, o_ref, z_ref, bb_ref, *, t, gs, tmn, scale):
    i = pl.program_id(1)
    n2 = t // FFT_RADIX
    d = x_ref.shape[-1]

    @pl.when(i == 0)
    def _():
        rstd = []
        for t1 in range(FFT_RADIX):
            xb = x_ref[t1 * n2:(t1 + 1) * n2, :]
            rstd.append(lax.rsqrt(jnp.mean(xb * xb, axis=-1, keepdims=True) + EPS))
        for q in range(d // gs):
            cols = slice(q * gs, (q + 1) * gs)
            w = []
            for t1 in range(FFT_RADIX):
                h = ((x_ref[t1 * n2:(t1 + 1) * n2, cols] * rstd[t1] * g_ref[:, cols]) * (1.0 + mod_ref[1:2, cols])
                     + mod_ref[0:1, cols])
                wc = _dot(h.astype(_MXU_DTYPE), wg_ref[...])
                w.append((wc[:, :gs], wc[:, gs:]))
            for k1, (re, im) in enumerate(_radix8(w)):
                bb_ref[k1, 0:n2, cols] = re.astype(bb_ref.dtype)
                bb_ref[k1, n2:2 * n2, cols] = im.astype(bb_ref.dtype)
        for k1 in range(FFT_RADIX):
            y = _dot(tab_ref[k1], bb_ref[k1])
            z_ref[k1 * n2:(k1 + 1) * n2, :] = (y * scale).astype(z_ref.dtype)

    runs = tmn // FFT_RADIX
    zs = jnp.concatenate(
        [z_ref[pl.ds(pl.multiple_of(k1 * n2 + i * runs, runs), runs), :] for k1 in range(FFT_RADIX)], axis=0)
    zn = _dot(pm_ref[...], zs).astype(_MXU_DTYPE)
    yx = _dot(zn, wo_ref[...])
    r0 = pl.multiple_of(i * tmn, tmn)
    o_ref[...] = x_ref[pl.ds(r0, tmn), :] + mod_ref[2:3, :] * yx


def _fourier_layer_fft(xa, mod, layer, norm_g, w_out):
    bsz, t, d = xa.shape
    mod_arr, mod_spec = mod.per_batch()
    zero2 = lambda b, i: (0, 0)
    gs = d // N_FOURIER_GROUPS
    tmn = min(t, 512)
    tab, wg, pm = _fft_tables(t, gs, tmn)
    n2 = t // FFT_RADIX
    scale = 1.0 / math.sqrt(t * gs)
    return pl.pallas_call(
        functools.partial(_fft_kernel, t=t, gs=gs, tmn=tmn, scale=scale),
        grid=(bsz, t // tmn),
        in_specs=[
            pl.BlockSpec((None, t, d), lambda b, i: (b, 0, 0)),
            mod_spec,
            _layer_spec((1, d), 2 * layer, zero2),
            _resident((FFT_RADIX, n2, 2 * n2), lambda b, i: (0, 0, 0)),
            _resident((gs, 2 * gs), zero2),
            _resident((tmn, tmn), zero2),
            _layer_spec((d, d), layer // 2, zero2, resident=True),
        ],
        out_specs=pl.BlockSpec((None, tmn, d), lambda b, i: (b, i, 0)),
        out_shape=jax.ShapeDtypeStruct(xa.shape, F32),
        scratch_shapes=[pltpu.VMEM((t, d), _MXU_DTYPE), pltpu.VMEM((FFT_RADIX, 2 * n2, d), _MXU_DTYPE)],
        compiler_params=_params("parallel", "arbitrary"),
        name="fourier_fft",
    )(xa, mod_arr, norm_g, tab, wg, pm, w_out)


def _fourier_mix_residual(xa, mod, layer, norm_g, w_out):
    t = xa.shape[1]
    n2 = t // FFT_RADIX
    if t % FFT_RADIX == 0 and n2 >= FFT_MIN_BLOCK and n2 % 16 == 0:
        return _fourier_layer_fft(xa, mod, layer, norm_g, w_out)
    return _fourier_layer(xa, mod, layer, norm_g, w_out)


class _RnnLayout:
    def __init__(self, n_blocks, rb):
        self.rb = rb
        self.n_blocks = n_blocks
        self.per_group = MXU_DIM // rb
        self.n_groups = -(-n_blocks // self.per_group)
        self.p = self.n_groups * MXU_DIM

    def pad_last(self, v):
        pg, rb = self.per_group, self.rb
        lead = v.shape[:-1]
        nl = len(lead)
        vb = v.reshape(lead + (self.n_blocks, rb))
        vb = jnp.pad(vb, [(0, 0)] * nl + [(0, self.n_groups * pg - self.n_blocks), (0, 0)])
        vg = vb.reshape(lead + (self.n_groups, pg * rb))
        vg = jnp.pad(vg, [(0, 0)] * nl + [(0, 0), (0, MXU_DIM - pg * rb)])
        return vg.reshape(lead + (self.p,))

    def block_diag(self, w):
        pg, rb = self.per_group, self.rb
        lead = w.shape[:-3]
        nl = len(lead)
        tot = self.n_groups * pg
        wp = jnp.pad(w, [(0, 0)] * nl + [(0, tot - self.n_blocks), (0, 0), (0, 0)])
        wp = wp.reshape(lead + (self.n_groups, pg, rb, 1, rb))
        eye = jnp.eye(pg, dtype=w.dtype).reshape(pg, 1, pg, 1)
        m = (wp * eye).reshape(lead + (self.n_groups, pg * rb, pg * rb))
        extra = MXU_DIM - pg * rb
        return jnp.pad(m, [(0, 0)] * (nl + 1) + [(0, extra), (0, extra)])


def _gelu_tanh(x):
    return 0.5 * x * (1.0 + jnp.tanh(math.sqrt(2.0 / math.pi) * (x + 0.044715 * (x * x * x))))


def _r1_kernel(x_ref, mod_ref, g_ref, win_ref, gate_ref, xr_ref, *, tt, p):
    x = x_ref[...]
    m = mod_ref[...]
    h = _norm_mod(x, g_ref[...], m[:, 0:1, :], m[:, 1:2, :])
    hb = h.reshape(SUBLANES * tt, h.shape[-1]).astype(_MXU_DTYPE)
    u = _dot(hb, win_ref[...])
    gate_ref[...] = _gelu_tanh(u[:, :p]).reshape(SUBLANES, tt, p).astype(gate_ref.dtype)
    for b in range(SUBLANES):
        for j in range(p // LANES):
            xr_ref[j, pl.ds(b, tt, stride=SUBLANES), :] = u[b * tt:(b + 1) * tt, p + j * LANES:p + (j + 1) * LANES]


def _rnn_in(xa, mod, layer, norm_g, win_p, p):
    bsz, t, d = xa.shape
    ng = bsz // SUBLANES
    tt = min(t, 64)
    ns = p // LANES
    mod_arr, mod_spec = mod.per_group()
    zero2 = lambda gi, i: (0, 0)
    return pl.pallas_call(
        functools.partial(_r1_kernel, tt=tt, p=p),
        grid=(ng, t // tt),
        in_specs=[
            pl.BlockSpec((SUBLANES, tt, d), lambda gi, i: (gi, i, 0)),
            mod_spec,
            _layer_spec((1, d), 2 * layer, zero2),
            _layer_spec((d, 2 * p), layer // 2, zero2, resident=True),
        ],
        out_specs=[
            pl.BlockSpec((SUBLANES, tt, p), lambda gi, i: (gi, i, 0)),
            pl.BlockSpec((None, ns, tt * SUBLANES, LANES), lambda gi, i: (gi, 0, i, 0)),
        ],
        out_shape=[
            jax.ShapeDtypeStruct((bsz, t, p), _STORE_DTYPE),
            jax.ShapeDtypeStruct((ng, ns, t * SUBLANES, LANES), F32),
        ],
        compiler_params=_params("parallel", "parallel"),
        name="rnn_in_proj",
    )(xa, mod_arr, norm_g, win_p)


def _softplus(z):
    return jnp.maximum(z, 0.0) + jnp.log(1.0 + jnp.exp(-jnp.abs(z)))


def _pack_bf16_pair(hi, lo):
    hb = lax.bitcast_convert_type(hi.astype(jnp.bfloat16).astype(F32), jnp.uint32)
    lb = lax.bitcast_convert_type(lo.astype(jnp.bfloat16).astype(F32), jnp.uint32)
    return hb | (lb >> 16)


def _unpack_bf16_pair(w):
    hi = lax.bitcast_convert_type(w & jnp.uint32(0xFFFF0000), F32)
    lo = lax.bitcast_convert_type(w << 16, F32)
    return hi, lo


def _r2_kernel(xm_ref, xp_ref, xn_ref, cw_ref, cb_ref, wg_ref, bg_ref, lam_ref, h0f_ref, h0b_ref,
               hs_ref, hff_ref, hfb_ref, ext_ref, a_ref, b_ref, hfall_ref, st_ref, *, tt, n_t):
    k = pl.program_id(2)
    rows = tt * SUBLANES
    halo_l = (CONV_W // 2) * SUBLANES
    halo_r = (CONV_W - 1 - CONV_W // 2) * SUBLANES
    gw = MXU_DIM

    def compute():
        c = jnp.where(k < n_t, k, 2 * n_t - 1 - k)
        for s in range(gw // LANES):
            ls = slice(s * LANES, (s + 1) * LANES)
            ext_ref[0:halo_l, ls] = jnp.where(c > 0, xp_ref[s], 0.0)
            ext_ref[halo_l:halo_l + rows, ls] = xm_ref[s]
            ext_ref[halo_l + rows:halo_l + rows + halo_r, ls] = jnp.where(c < n_t - 1, xn_ref[s], 0.0)
        xh = cb_ref[...] + sum(
            cw_ref[kk:kk + 1, :] * ext_ref[kk * SUBLANES:kk * SUBLANES + rows, :] for kk in range(CONV_W))
        p = _dot(xh.astype(_MXU_DTYPE), wg_ref[...]) + bg_ref[...]
        th_r = jnp.tanh(p[:, :gw])
        th_i = jnp.tanh(p[:, gw:])
        kh = (-0.5 * RG_C * math.log2(math.e)) * _softplus(-lam_ref[...])
        a = jnp.exp2(kh * th_r + kh)
        y = 1.0 - a * a
        a_ref[...] = a
        b_ref[...] = (y * lax.rsqrt(jnp.maximum(y, 1e-30))) * ((th_i + 1.0) * xh)

    def scan(fwd):
        j = k - 1
        if fwd:
            base = j * rows
            h = jnp.where(j == 0, h0f_ref[...], st_ref[...])
        else:
            base = (2 * n_t - 1 - j) * rows
            h = jnp.where(j == n_t, h0b_ref[...], st_ref[...])
        for t in (range(tt) if fwd else reversed(range(tt))):
            o = t * SUBLANES
            h = a_ref[o:o + SUBLANES, :] * h + b_ref[o:o + SUBLANES, :]
            hrow = pl.ds(pl.multiple_of(base + o, SUBLANES), SUBLANES)
            if fwd:
                hfall_ref[hrow, :] = h
            else:
                tot = hfall_ref[hrow, :] + h
                hs_ref[o:o + SUBLANES, :] = _pack_bf16_pair(tot[:, :LANES], tot[:, LANES:])
        st_ref[...] = h

    @pl.when(k == 0)
    def _():
        compute()

    @pl.when(jnp.logical_and(k >= 1, k <= n_t))
    def _():
        scan(True)
        compute()

    @pl.when(k == n_t)
    def _():
        hff_ref[...] = st_ref[...]

    @pl.when(jnp.logical_and(k > n_t, k < 2 * n_t))
    def _():
        scan(False)
        compute()

    @pl.when(k == 2 * n_t)
    def _():
        scan(False)
        hfb_ref[...] = st_ref[...]


def _rnn_scan(xr, jr, conv_w, conv_b, wg, bg, lam, h0f, h0b, t):
    ng, ns, trows, _ = xr.shape
    p = ns * LANES
    nq = p // MXU_DIM
    spg = MXU_DIM // LANES
    assert spg == 2
    tt = min(t, 128)
    n_t = t // tt
    rows = tt * SUBLANES
    halo_l = (CONV_W // 2) * SUBLANES
    halo_r = (CONV_W - 1 - CONV_W // 2) * SUBLANES

    def chunk(k):
        kk = jnp.minimum(k, 2 * n_t - 1)
        return jnp.where(kk < n_t, kk, 2 * n_t - 1 - kk)

    def out_chunk(k):
        return jnp.where(k <= n_t, n_t - 1, 2 * n_t - k)

    def phase(k):
        return jnp.where(k < n_t, 0, 1)

    return pl.pallas_call(
        functools.partial(_r2_kernel, tt=tt, n_t=n_t),
        grid=(ng, nq, 2 * n_t + 1),
        in_specs=[
            pl.BlockSpec((None, spg, rows, LANES), lambda g, q, k: (g, q, chunk(k), 0)),
            pl.BlockSpec((None, spg, halo_l, LANES),
                         lambda g, q, k: (g, q, jnp.maximum(chunk(k) * (rows // halo_l) - 1, 0), 0)),
            pl.BlockSpec((None, spg, halo_r, LANES),
                         lambda g, q, k: (g, q, jnp.minimum((chunk(k) + 1) * (rows // halo_r), trows // halo_r - 1), 0)),
            _layer_spec((CONV_W, MXU_DIM), jr, lambda g, q, k: (0, q)),
            _layer_spec((1, MXU_DIM), jr, lambda g, q, k: (0, q)),
            _layer_spec((None, None, MXU_DIM, 2 * MXU_DIM), jr, lambda g, q, k: (phase(k), q, 0, 0)),
            _layer_spec((None, None, 1, 2 * MXU_DIM), jr, lambda g, q, k: (phase(k), q, 0, 0)),
            _layer_spec((None, 1, MXU_DIM), jr, lambda g, q, k: (phase(k), 0, q)),
            pl.BlockSpec((None, SUBLANES, MXU_DIM), lambda g, q, k: (g, 0, q)),
            pl.BlockSpec((None, SUBLANES, MXU_DIM), lambda g, q, k: (g, 0, q)),
        ],
        out_specs=[
            pl.BlockSpec((None, None, rows, LANES), lambda g, q, k: (g, q, out_chunk(k), 0)),
            pl.BlockSpec((None, SUBLANES, MXU_DIM), lambda g, q, k: (g, 0, q)),
            pl.BlockSpec((None, SUBLANES, MXU_DIM), lambda g, q, k: (g, 0, q)),
        ],
        out_shape=[
            jax.ShapeDtypeStruct((ng, nq, trows, LANES), jnp.uint32),
            jax.ShapeDtypeStruct((ng, SUBLANES, p), F32),
            jax.ShapeDtypeStruct((ng, SUBLANES, p), F32),
        ],
        scratch_shapes=[
            pltpu.VMEM((halo_l + rows + halo_r, MXU_DIM), F32),
            pltpu.VMEM((rows, MXU_DIM), F32),
            pltpu.VMEM((rows, MXU_DIM), F32),
            pltpu.VMEM((trows, MXU_DIM), F32),
            pltpu.VMEM((SUBLANES, MXU_DIM), F32),
        ],
        compiler_params=_params("parallel", "parallel", "arbitrary"),
        name="rnn_scan",
    )(xr, xr, xr, conv_w, conv_b, wg, bg, lam, h0f, h0b)


def _r3_kernel(x_ref, hs_ref, gate_ref, mod_ref, wo_ref, o_ref, z_ref, *, tt, p):
    for b in range(SUBLANES):
        for q in range(p // MXU_DIM):
            halves = _unpack_bf16_pair(hs_ref[q, pl.ds(b, tt, stride=SUBLANES), :])
            for s, hv in enumerate(halves):
                cols = slice(q * MXU_DIM + s * LANES, q * MXU_DIM + (s + 1) * LANES)
                z_ref[b * tt:(b + 1) * tt, cols] = (hv * gate_ref[b, :, cols].astype(F32)).astype(z_ref.dtype)
    yx = _dot(z_ref[...], wo_ref[...])
    d = yx.shape[-1]
    o_ref[...] = x_ref[...] + mod_ref[:, 2:3, :] * yx.reshape(SUBLANES, tt, d)


def _rnn_out(xa, hs, gate, mod, layer, wout_p):
    bsz, t, d = xa.shape
    mod_arr, mod_spec = mod.per_group()
    p = gate.shape[-1]
    ng = bsz // SUBLANES
    nq = p // MXU_DIM
    tt = min(t, 64)
    return pl.pallas_call(
        functools.partial(_r3_kernel, tt=tt, p=p),
        grid=(ng, t // tt),
        in_specs=[
            pl.BlockSpec((SUBLANES, tt, d), lambda gi, i: (gi, i, 0)),
            pl.BlockSpec((None, nq, tt * SUBLANES, LANES), lambda gi, i: (gi, 0, i, 0)),
            pl.BlockSpec((SUBLANES, tt, p), lambda gi, i: (gi, i, 0)),
            mod_spec,
            _layer_spec((p, d), layer // 2, lambda gi, i: (0, 0), resident=True),
        ],
        out_specs=pl.BlockSpec((SUBLANES, tt, d), lambda gi, i: (gi, i, 0)),
        out_shape=jax.ShapeDtypeStruct(xa.shape, F32),
        scratch_shapes=[pltpu.VMEM((SUBLANES * tt, p), _MXU_DTYPE)],
        compiler_params=_params("parallel", "parallel"),
        name="rnn_out_proj",
    )(xa, hs, gate, mod_arr, wout_p)


def _rnn_weights(lay, w_in, conv_w, conv_b, w_a, b_a, w_i, b_i, lam, w_out):
    nr = w_in.shape[0]
    d_rnn = conv_b.shape[-1]
    win_p = jnp.concatenate([lay.pad_last(w_in[..., :d_rnn]), lay.pad_last(w_in[..., d_rnn:])], axis=-1).astype(_MXU_DTYPE)
    cw_p = 0.5 * lay.pad_last(conv_w)
    cb_p = 0.5 * lay.pad_last(conv_b).reshape(nr, 1, lay.p)
    wg = jnp.concatenate([lay.block_diag(w_a), lay.block_diag(w_i)], axis=-1).astype(_MXU_DTYPE)
    bg = 0.5 * jnp.concatenate([lay.pad_last(b_a).reshape(nr, 2, lay.n_groups, 1, MXU_DIM),
                                lay.pad_last(b_i).reshape(nr, 2, lay.n_groups, 1, MXU_DIM)], axis=-1)
    lam_p = lay.pad_last(lam).reshape(nr, 2, 1, lay.p)
    wout_p = jnp.swapaxes(lay.pad_last(jnp.swapaxes(w_out, 1, 2)), 1, 2).astype(_MXU_DTYPE)
    return win_p, cw_p, cb_p, wg, bg, lam_p, wout_p


def kernel(x, c, ctx, c_ctx, w_mod, b_mod, norm_g, w_fourier, w_rnn_in, conv_w, conv_b,
           w_a, b_a, w_i, b_i, lam, w_rnn_out, w1, b1, w2, b2, final_g):
    bsz, seq, d = x.shape
    depth = w_mod.shape[0]
    assert bsz % SUBLANES == 0 and d % (N_FOURIER_GROUPS * LANES) == 0
    lay = _RnnLayout(w_a.shape[2], w_a.shape[-1])
    ng = bsz // SUBLANES

    pad_rows = (-(bsz + 1)) % SUBLANES
    cv = jnp.concatenate([c, c_ctx[None, :], jnp.zeros((pad_rows, d), F32)], axis=0)
    mod = _modulation(cv, w_mod, b_mod).reshape(depth, bsz + 1 + pad_rows, N_MOD, d)
    mod_c8 = jnp.broadcast_to(mod[:, bsz:bsz + 1], (depth, SUBLANES, N_MOD, d))

    w1b, w2b = w1.astype(_MXU_DTYPE), w2.astype(_MXU_DTYPE)
    wfb = w_fourier.astype(_MXU_DTYPE)
    ng2 = norm_g.reshape(depth * 2, 1, d)
    b1r, b2r = b1.reshape(depth, 1, -1), b2.reshape(depth, 1, d)
    win_p, cw_p, cb_p, wg, bg, lam_p, wout_p = _rnn_weights(
        lay, w_rnn_in, conv_w, conv_b, w_a, b_a, w_i, b_i, lam, w_rnn_out)
    zeros = jnp.zeros((ng, SUBLANES, lay.p), F32)

    for i in range(depth):
        last = i == depth - 1
        mx = _Mod(mod, mod_c8, i, False, bsz)
        mc = _Mod(mod, mod_c8, i, True, bsz)
        if i % 2 == 0:
            x = _fourier_mix_residual(x, mx, i, ng2, wfb)
            if not last:
                ctx = _fourier_mix_residual(ctx, mc, i, ng2, wfb)
        else:
            jr = i // 2
            gate_c, xr_c = _rnn_in(ctx, mc, i, ng2, win_p, lay.p)
            hs_c, hf_c, hb_c = _rnn_scan(xr_c, jr, cw_p, cb_p, wg, bg, lam_p, zeros, zeros, ctx.shape[1])
            gate_x, xr_x = _rnn_in(x, mx, i, ng2, win_p, lay.p)
            hs_x, _, _ = _rnn_scan(xr_x, jr, cw_p, cb_p, wg, bg, lam_p, hf_c, hb_c, seq)
            x = _rnn_out(x, hs_x, gate_x, mx, i, wout_p)
            if not last:
                ctx = _rnn_out(ctx, hs_c, gate_c, mc, i, wout_p)
        x = _mlp(x, mx, i, ng2, w1b, b1r, w2b, b2r, final_g if last else None)
        if not last:
            ctx = _mlp(ctx, mc, i, ng2, w1b, b1r, w2b, b2r)
    return x
```

```python
import functools
import math

import numpy as np
import jax
import jax.numpy as jnp
from jax import lax
from jax.experimental import pallas as pl
from jax.experimental.pallas import tpu as pltpu

EPS = 1e-6
RG_C = 8.0
N_FOURIER_GROUPS = 4
CONV_W = 4
N_MOD = 6

LANES = 128
SUBLANES = 8
MXU_DIM = 256
VMEM_LIMIT = 56 << 20
_MXU_DTYPE = jnp.bfloat16
_STORE_DTYPE = jnp.bfloat16
F32 = jnp.float32


def _params(*sem):
    return pltpu.CompilerParams(dimension_semantics=sem, vmem_limit_bytes=VMEM_LIMIT)


def _resident(shape, index_map):
    return pl.BlockSpec(shape, index_map, pipeline_mode=pl.Buffered(1))


def _dot(a, b):
    return jnp.dot(a, b, preferred_element_type=F32)


def _layer_spec(block, layer, index_map, resident=False):
    make = _resident if resident else pl.BlockSpec
    return make((None,) + tuple(block), lambda *ids: (layer,) + tuple(index_map(*ids)))


class _Mod:
    def __init__(self, table, table_ctx8, layer, is_ctx, bsz):
        self.table, self.table_ctx8, self.layer, self.is_ctx, self.bsz = table, table_ctx8, layer, is_ctx, bsz

    def per_batch(self):
        d = self.table.shape[-1]
        layer, fixed = self.layer, (self.bsz if self.is_ctx else None)
        return self.table, pl.BlockSpec(
            (None, None, N_MOD, d), lambda b, i: (layer, b if fixed is None else fixed, 0, 0))

    def per_group(self):
        d = self.table.shape[-1]
        layer = self.layer
        if self.is_ctx:
            return self.table_ctx8, pl.BlockSpec((None, SUBLANES, N_MOD, d), lambda gi, i: (layer, 0, 0, 0))
        return self.table, pl.BlockSpec((None, SUBLANES, N_MOD, d), lambda gi, i: (layer, gi, 0, 0))


def _norm_mod(x, g, shift, scale):
    ms = jnp.mean(x * x, axis=-1, keepdims=True)
    y = x * lax.rsqrt(ms + EPS) * g
    return y * (1.0 + scale) + shift


def _mod_kernel(cv_ref, w_ref, b_ref, o_ref):
    c = cv_ref[...]
    s = c * jax.nn.sigmoid(c)
    o_ref[...] = _dot(s.astype(_MXU_DTYPE), w_ref[...].astype(_MXU_DTYPE)) + b_ref[...]


def _modulation(cv, w_mod, b_mod):
    depth, d, n = w_mod.shape
    rows = cv.shape[0]
    tn = min(n, 1024)
    return pl.pallas_call(
        _mod_kernel,
        grid=(depth, n // tn),
        in_specs=[
            pl.BlockSpec((rows, d), lambda i, j: (0, 0)),
            pl.BlockSpec((None, d, tn), lambda i, j: (i, 0, j)),
            pl.BlockSpec((None, 1, tn), lambda i, j: (i, 0, j)),
        ],
        out_specs=pl.BlockSpec((None, rows, tn), lambda i, j: (i, 0, j)),
        out_shape=jax.ShapeDtypeStruct((depth, rows, n), F32),
        compiler_params=_params("parallel", "parallel"),
        name="modulation",
    )(cv, w_mod, b_mod.reshape(depth, 1, n))


def _mlp_kernel(x_ref, mod_ref, g_ref, w1_ref, b1_ref, w2_ref, b2_ref, *rest, tf, final):
    if final:
        fg_ref, o_ref = rest
    else:
        (o_ref,) = rest
    x = x_ref[...]
    h = _norm_mod(x, g_ref[...], mod_ref[3:4, :], mod_ref[4:5, :]).astype(_MXU_DTYPE)
    dff = w1_ref.shape[1]
    acc = jnp.zeros(x.shape, F32)
    for j in range(dff // tf):
        a = _dot(h, w1_ref[:, j * tf:(j + 1) * tf]) + b1_ref[:, j * tf:(j + 1) * tf]
        a = jnp.maximum(a, 0.0)
        a = a * a
        acc = acc + _dot(a.astype(_MXU_DTYPE), w2_ref[j * tf:(j + 1) * tf, :])
    out = x + mod_ref[5:6, :] * (acc + b2_ref[...])
    if final:
        ms = jnp.mean(out * out, axis=-1, keepdims=True)
        out = out * lax.rsqrt(ms + EPS) * fg_ref[...]
    o_ref[...] = out


def _mlp(xa, mod, layer, norm_g, w1, b1, w2, b2, final_g=None):
    bsz, t, d = xa.shape
    dff = w1.shape[-1]
    tm = min(t, 512)
    tf = min(dff, 1024)
    final = final_g is not None
    mod_arr, mod_spec = mod.per_batch()
    zero2 = lambda b, i: (0, 0)
    in_specs = [
        pl.BlockSpec((None, tm, d), lambda b, i: (b, i, 0)),
        mod_spec,
        _layer_spec((1, d), 2 * layer + 1, zero2),
        _layer_spec((d, dff), layer, zero2, resident=True),
        _layer_spec((1, dff), layer, zero2),
        _layer_spec((dff, d), layer, zero2, resident=True),
        _layer_spec((1, d), layer, zero2),
    ]
    args = [xa, mod_arr, norm_g, w1, b1, w2, b2]
    if final:
        in_specs.append(pl.BlockSpec((1, d), zero2))
        args.append(final_g.reshape(1, d))
    return pl.pallas_call(
        functools.partial(_mlp_kernel, tf=tf, final=final),
        grid=(bsz, t // tm),
        in_specs=in_specs,
        out_specs=pl.BlockSpec((None, tm, d), lambda b, i: (b, i, 0)),
        out_shape=jax.ShapeDtypeStruct(xa.shape, F32),
        compiler_params=_params("parallel", "parallel"),
        name="mlp",
    )(*args)


def _dft_tables(t, gs):
    c = np.arange(gs, dtype=np.int64)
    ang = 2.0 * np.pi * ((c[:, None] * c[None, :]) % gs) / gs
    wg = np.concatenate([np.cos(ang), -np.sin(ang)], axis=1)
    k = np.arange(t, dtype=np.int64)
    angt = 2.0 * np.pi * ((k[:, None] * k[None, :]) % t) / t
    cs = np.stack([np.cos(angt), np.sin(angt)])
    return jnp.asarray(wg, _MXU_DTYPE), jnp.asarray(cs, _MXU_DTYPE)


def _f1_kernel(x_ref, mod_ref, g_ref, wg_ref, o_ref, *, gs):
    h = _norm_mod(x_ref[...], g_ref[...], mod_ref[0:1, :], mod_ref[1:2, :]).astype(_MXU_DTYPE)
    for q in range(h.shape[-1] // gs):
        res = _dot(h[:, q * gs:(q + 1) * gs], wg_ref[...])
        o_ref[0, :, q * gs:(q + 1) * gs] = res[:, :gs].astype(o_ref.dtype)
        o_ref[1, :, q * gs:(q + 1) * gs] = res[:, gs:].astype(o_ref.dtype)


def _f2_kernel(x_ref, g2_ref, cs_ref, mod_ref, wo_ref, o_ref, *, tm, scale):
    r0 = pl.multiple_of(pl.program_id(1) * tm, tm)
    y = _dot(cs_ref[0, pl.ds(r0, tm), :], g2_ref[0]) + _dot(cs_ref[1, pl.ds(r0, tm), :], g2_ref[1])
    yx = _dot((y * scale).astype(_MXU_DTYPE), wo_ref[...])
    o_ref[...] = x_ref[...] + mod_ref[2:3, :] * yx


def _fourier_layer(xa, mod, layer, norm_g, w_out):
    bsz, t, d = xa.shape
    mod_arr, mod_spec = mod.per_batch()
    zero2 = lambda b, i: (0, 0)
    jf = layer // 2
    gs = d // N_FOURIER_GROUPS
    tm = min(t, 512)
    wg, cs = _dft_tables(t, gs)
    g2 = pl.pallas_call(
        functools.partial(_f1_kernel, gs=gs),
        grid=(bsz, t // tm),
        in_specs=[
            pl.BlockSpec((None, tm, d), lambda b, i: (b, i, 0)),
            mod_spec,
            _layer_spec((1, d), 2 * layer, zero2),
            pl.BlockSpec((gs, 2 * gs), zero2),
        ],
        out_specs=pl.BlockSpec((None, 2, tm, d), lambda b, i: (b, 0, i, 0)),
        out_shape=jax.ShapeDtypeStruct((bsz, 2, t, d), _MXU_DTYPE),
        compiler_params=_params("parallel", "parallel"),
        name="fourier_group_dft",
    )(xa, mod_arr, norm_g, wg)
    scale = 1.0 / math.sqrt(t * gs)
    return pl.pallas_call(
        functools.partial(_f2_kernel, tm=tm, scale=scale),
        grid=(bsz, t // tm),
        in_specs=[
            pl.BlockSpec((None, tm, d), lambda b, i: (b, i, 0)),
            pl.BlockSpec((None, 2, t, d), lambda b, i: (b, 0, 0, 0)),
            _resident((2, t, t), lambda b, i: (0, 0, 0)),
            mod_spec,
            _layer_spec((d, d), jf, zero2, resident=True),
        ],
        out_specs=pl.BlockSpec((None, tm, d), lambda b, i: (b, i, 0)),
        out_shape=jax.ShapeDtypeStruct(xa.shape, F32),
        compiler_params=_params("parallel", "arbitrary"),
        name="fourier_time_dft",
    )(xa, g2, cs, mod_arr, w_out)


FFT_RADIX = 8
FFT_MIN_BLOCK = MXU_DIM


def _fft_tables(t, gs, tmn):
    n2 = t // FFT_RADIX
    k2 = np.arange(n2, dtype=np.int64)
    tabs = []
    for k1 in range(FFT_RADIX):
        k = k1 + FFT_RADIX * k2
        ang = 2.0 * np.pi * ((k[:, None] * k2[None, :]) % t) / t
        tabs.append(np.concatenate([np.cos(ang), np.sin(ang)], axis=1))
    c = np.arange(gs, dtype=np.int64)
    angc = 2.0 * np.pi * ((c[:, None] * c[None, :]) % gs) / gs
    wg = np.concatenate([np.cos(angc), -np.sin(angc)], axis=1)
    runs = tmn // FFT_RADIX
    pm = np.zeros((tmn, tmn))
    m = np.arange(runs)
    for k1 in range(FFT_RADIX):
        pm[FFT_RADIX * m + k1, k1 * runs + m] = 1.0
    return (jnp.asarray(np.stack(tabs), _MXU_DTYPE), jnp.asarray(wg, _MXU_DTYPE), jnp.asarray(pm, _MXU_DTYPE))


def _dft4(v):
    (ar, ai), (br, bi), (cr, ci), (dr, di) = v
    e0r, e0i, e1r, e1i = ar + cr, ai + ci, ar - cr, ai - ci
    f0r, f0i, f1r, f1i = br + dr, bi + di, br - dr, bi - di
    return [(e0r + f0r, e0i + f0i), (e1r + f1i, e1i - f1r), (e0r - f0r, e0i - f0i), (e1r - f1i, e1i + f1r)]


def _radix8(w):
    s = [(w[i][0] + w[i + 4][0], w[i][1] + w[i + 4][1]) for i in range(4)]
    dd = [(w[i][0] - w[i + 4][0], w[i][1] - w[i + 4][1]) for i in range(4)]
    rt = math.sqrt(0.5)
    tw = [dd[0],
          ((dd[1][0] + dd[1][1]) * rt, (dd[1][1] - dd[1][0]) * rt),
          (dd[2][1], -dd[2][0]),
          ((dd[3][1] - dd[3][0]) * rt, -(dd[3][0] + dd[3][1]) * rt)]
    even, odd = _dft4(s), _dft4(tw)
    return [even[0], odd[0], even[1], odd[1], even[2], odd[2], even[3], odd[3]]


def _fft_kernel(x_ref, mod_ref, g_ref, tab_ref, wg_ref, pm_ref, wo_ref, o_ref, z_ref, bb_ref, *, t, gs, tmn, scale):
    i = pl.program_id(1)
    n2 = t // FFT_RADIX
    d = x_ref.shape[-1]

    @pl.when(i == 0)
    def _():
        rstd = []
        for t1 in range(FFT_RADIX):
            xb = x_ref[t1 * n2:(t1 + 1) * n2, :]
            rstd.append(lax.rsqrt(jnp.mean(xb * xb, axis=-1, keepdims=True) + EPS))
        for q in range(d // gs):
            cols = slice(q * gs, (q + 1) * gs)
            w = []
            for t1 in range(FFT_RADIX):
                h = ((x_ref[t1 * n2:(t1 + 1) * n2, cols] * rstd[t1] * g_ref[:, cols]) * (1.0 + mod_ref[1:2, cols])
                     + mod_ref[0:1, cols])
                wc = _dot(h.astype(_MXU_DTYPE), wg_ref[...])
                w.append((wc[:, :gs], wc[:, gs:]))
            for k1, (re, im) in enumerate(_radix8(w)):
                bb_ref[k1, 0:n2, cols] = re.astype(bb_ref.dtype)
                bb_ref[k1, n2:2 * n2, cols] = im.astype(bb_ref.dtype)
        for k1 in range(FFT_RADIX):
            y = _dot(tab_ref[k1], bb_ref[k1])
            z_ref[k1 * n2:(k1 + 1) * n2, :] = (y * scale).astype(z_ref.dtype)

    runs = tmn // FFT_RADIX
    zs = jnp.concatenate(
        [z_ref[pl.ds(pl.multiple_of(k1 * n2 + i * runs, runs), runs), :] for k1 in range(FFT_RADIX)], axis=0)
    zn = _dot(pm_ref[...], zs).astype(_MXU_DTYPE)
    yx = _dot(zn, wo_ref[...])
    r0 = pl.multiple_of(i * tmn, tmn)
    o_ref[...] = x_ref[pl.ds(r0, tmn), :] + mod_ref[2:3, :] * yx


def _fourier_layer_fft(xa, mod, layer, norm_g, w_out):
    bsz, t, d = xa.shape
    mod_arr, mod_spec = mod.per_batch()
    zero2 = lambda b, i: (0, 0)
    gs = d // N_FOURIER_GROUPS
    tmn = min(t, 512)
    tab, wg, pm = _fft_tables(t, gs, tmn)
    n2 = t // FFT_RADIX
    scale = 1.0 / math.sqrt(t * gs)
    return pl.pallas_call(
        functools.partial(_fft_kernel, t=t, gs=gs, tmn=tmn, scale=scale),
        grid=(bsz, t // tmn),
        in_specs=[
            pl.BlockSpec((None, t, d), lambda b, i: (b, 0, 0)),
            mod_spec,
            _layer_spec((1, d), 2 * layer, zero2),
            _resident((FFT_RADIX, n2, 2 * n2), lambda b, i: (0, 0, 0)),
            _resident((gs, 2 * gs), zero2),
            _resident((tmn, tmn), zero2),
            _layer_spec((d, d), layer // 2, zero2, resident=True),
        ],
        out_specs=pl.BlockSpec((None, tmn, d), lambda b, i: (b, i, 0)),
        out_shape=jax.ShapeDtypeStruct(xa.shape, F32),
        scratch_shapes=[pltpu.VMEM((t, d), _MXU_DTYPE), pltpu.VMEM((FFT_RADIX, 2 * n2, d), _MXU_DTYPE)],
        compiler_params=_params("parallel", "arbitrary"),
        name="fourier_fft",
    )(xa, mod_arr, norm_g, tab, wg, pm, w_out)


def _fourier_mix_residual(xa, mod, layer, norm_g, w_out):
    t = xa.shape[1]
    n2 = t // FFT_RADIX
    if t % FFT_RADIX == 0 and n2 >= FFT_MIN_BLOCK and n2 % 16 == 0:
        return _fourier_layer_fft(xa, mod, layer, norm_g, w_out)
    return _fourier_layer(xa, mod, layer, norm_g, w_out)


class _RnnLayout:
    def __init__(self, n_blocks, rb):
        self.rb = rb
        self.n_blocks = n_blocks
        self.per_group = MXU_DIM // rb
        self.n_groups = -(-n_blocks // self.per_group)
        self.p = self.n_groups * MXU_DIM

    def pad_last(self, v):
        pg, rb = self.per_group, self.rb
        lead = v.shape[:-1]
        nl = len(lead)
        vb = v.reshape(lead + (self.n_blocks, rb))
        vb = jnp.pad(vb, [(0, 0)] * nl + [(0, self.n_groups * pg - self.n_blocks), (0, 0)])
        vg = vb.reshape(lead + (self.n_groups, pg * rb))
        vg = jnp.pad(vg, [(0, 0)] * nl + [(0, 0), (0, MXU_DIM - pg * rb)])
        return vg.reshape(lead + (self.p,))

    def block_diag(self, w):
        pg, rb = self.per_group, self.rb
        lead = w.shape[:-3]
        nl = len(lead)
        tot = self.n_groups * pg
        wp = jnp.pad(w, [(0, 0)] * nl + [(0, tot - self.n_blocks), (0, 0), (0, 0)])
        wp = wp.reshape(lead + (self.n_groups, pg, rb, 1, rb))
        eye = jnp.eye(pg, dtype=w.dtype).reshape(pg, 1, pg, 1)
        m = (wp * eye).reshape(lead + (self.n_groups, pg * rb, pg * rb))
        extra = MXU_DIM - pg * rb
        return jnp.pad(m, [(0, 0)] * (nl + 1) + [(0, extra), (0, extra)])


def _gelu_tanh(x):
    return 0.5 * x * (1.0 + jnp.tanh(math.sqrt(2.0 / math.pi) * (x + 0.044715 * (x * x * x))))


def _r1_kernel(x_ref, mod_ref, g_ref, win_ref, gate_ref, xr_ref, *, tt, p):
    x = x_ref[...]
    m = mod_ref[...]
    h = _norm_mod(x, g_ref[...], m[:, 0:1, :], m[:, 1:2, :])
    hb = h.reshape(SUBLANES * tt, h.shape[-1]).astype(_MXU_DTYPE)
    u = _dot(hb, win_ref[...])
    gate_ref[...] = u[:, :p].reshape(SUBLANES, tt, p).astype(gate_ref.dtype)
    for b in range(SUBLANES):
        for j in range(p // LANES):
            xr_ref[j, pl.ds(b, tt, stride=SUBLANES), :] = u[b * tt:(b + 1) * tt, p + j * LANES:p + (j + 1) * LANES]


def _rnn_in(xa, mod, layer, norm_g, win_p, p):
    bsz, t, d = xa.shape
    ng = bsz // SUBLANES
    tt = min(t, 64)
    ns = p // LANES
    mod_arr, mod_spec = mod.per_group()
    zero2 = lambda gi, i: (0, 0)
    return pl.pallas_call(
        functools.partial(_r1_kernel, tt=tt, p=p),
        grid=(ng, t // tt),
        in_specs=[
            pl.BlockSpec((SUBLANES, tt, d), lambda gi, i: (gi, i, 0)),
            mod_spec,
            _layer_spec((1, d), 2 * layer, zero2),
            _layer_spec((d, 2 * p), layer // 2, zero2, resident=True),
        ],
        out_specs=[
            pl.BlockSpec((SUBLANES, tt, p), lambda gi, i: (gi, i, 0)),
            pl.BlockSpec((None, ns, tt * SUBLANES, LANES), lambda gi, i: (gi, 0, i, 0)),
        ],
        out_shape=[
            jax.ShapeDtypeStruct((bsz, t, p), _STORE_DTYPE),
            jax.ShapeDtypeStruct((ng, ns, t * SUBLANES, LANES), F32),
        ],
        compiler_params=_params("parallel", "parallel"),
        name="rnn_in_proj",
    )(xa, mod_arr, norm_g, win_p)


SCAN_CHUNK = 256


def _softplus(z):
    return jnp.maximum(z, 0.0) + jnp.log(1.0 + jnp.exp(-jnp.abs(z)))


def _pack_bf16_pair(hi, lo):
    hb = lax.bitcast_convert_type(hi.astype(jnp.bfloat16).astype(F32), jnp.uint32)
    lb = lax.bitcast_convert_type(lo.astype(jnp.bfloat16).astype(F32), jnp.uint32)
    return hb | (lb >> 16)


def _unpack_bf16_pair(w):
    hi = lax.bitcast_convert_type(w & jnp.uint32(0xFFFF0000), F32)
    lo = lax.bitcast_convert_type(w << 16, F32)
    return hi, lo


def _r2_kernel(xm_ref, xp_ref, xn_ref, cw_ref, cb_ref, wg_ref, bg_ref, lam_ref, h0f_ref, h0b_ref,
               hs_ref, hff_ref, hfb_ref, ext_ref, a_ref, b_ref, hfall_ref, st_ref, *, tt, n_t):
    k = pl.program_id(2)
    rows = tt * SUBLANES
    halo_l = (CONV_W // 2) * SUBLANES
    halo_r = (CONV_W - 1 - CONV_W // 2) * SUBLANES
    gw = MXU_DIM

    def compute():
        c = jnp.where(k < n_t, k, 2 * n_t - 1 - k)
        for s in range(gw // LANES):
            ls = slice(s * LANES, (s + 1) * LANES)
            ext_ref[0:halo_l, ls] = jnp.where(c > 0, xp_ref[s], 0.0)
            ext_ref[halo_l:halo_l + rows, ls] = xm_ref[s]
            ext_ref[halo_l + rows:halo_l + rows + halo_r, ls] = jnp.where(c < n_t - 1, xn_ref[s], 0.0)
        xh = cb_ref[...] + sum(
            cw_ref[kk:kk + 1, :] * ext_ref[kk * SUBLANES:kk * SUBLANES + rows, :] for kk in range(CONV_W))
        p = _dot(xh.astype(_MXU_DTYPE), wg_ref[...]) + bg_ref[...]
        th_r = jnp.tanh(p[:, :gw])
        th_i = jnp.tanh(p[:, gw:])
        kh = (-0.5 * RG_C * math.log2(math.e)) * _softplus(-lam_ref[...])
        a = jnp.exp2(kh * th_r + kh)
        y = 1.0 - a * a
        a_ref[...] = a
        b_ref[...] = (y * lax.rsqrt(jnp.maximum(y, 1e-30))) * ((th_i + 1.0) * xh)

    def scan(fwd):
        j = k - 1
        if fwd:
            base = j * rows
            h = jnp.where(j == 0, h0f_ref[...], st_ref[...])
        else:
            base = (2 * n_t - 1 - j) * rows
            h = jnp.where(j == n_t, h0b_ref[...], st_ref[...])
        for t in (range(tt) if fwd else reversed(range(tt))):
            o = t * SUBLANES
            h = a_ref[o:o + SUBLANES, :] * h + b_ref[o:o + SUBLANES, :]
            hrow = pl.ds(pl.multiple_of(base + o, SUBLANES), SUBLANES)
            if fwd:
                hfall_ref[hrow, :] = h
            else:
                tot = hfall_ref[hrow, :] + h
                hs_ref[o:o + SUBLANES, :] = _pack_bf16_pair(tot[:, :LANES], tot[:, LANES:])
        st_ref[...] = h

    @pl.when(k == 0)
    def _():
        compute()

    @pl.when(jnp.logical_and(k >= 1, k <= n_t))
    def _():
        scan(True)
        compute()

    @pl.when(k == n_t)
    def _():
        hff_ref[...] = st_ref[...]

    @pl.when(jnp.logical_and(k > n_t, k < 2 * n_t))
    def _():
        scan(False)
        compute()

    @pl.when(k == 2 * n_t)
    def _():
        scan(False)
        hfb_ref[...] = st_ref[...]


def _rnn_scan(xr, jr, conv_w, conv_b, wg, bg, lam, h0f, h0b, t):
    ng, ns, trows, _ = xr.shape
    p = ns * LANES
    nq = p // MXU_DIM
    spg = MXU_DIM // LANES
    assert spg == 2
    tt = min(t, SCAN_CHUNK)
    n_t = t // tt
    rows = tt * SUBLANES
    halo_l = (CONV_W // 2) * SUBLANES
    halo_r = (CONV_W - 1 - CONV_W // 2) * SUBLANES

    def chunk(k):
        kk = jnp.minimum(k, 2 * n_t - 1)
        return jnp.where(kk < n_t, kk, 2 * n_t - 1 - kk)

    def out_chunk(k):
        return jnp.where(k <= n_t, n_t - 1, 2 * n_t - k)

    def phase(k):
        return jnp.where(k < n_t, 0, 1)

    return pl.pallas_call(
        functools.partial(_r2_kernel, tt=tt, n_t=n_t),
        grid=(ng, nq, 2 * n_t + 1),
        in_specs=[
            pl.BlockSpec((None, spg, rows, LANES), lambda g, q, k: (g, q, chunk(k), 0)),
            pl.BlockSpec((None, spg, halo_l, LANES),
                         lambda g, q, k: (g, q, jnp.maximum(chunk(k) * (rows // halo_l) - 1, 0), 0)),
            pl.BlockSpec((None, spg, halo_r, LANES),
                         lambda g, q, k: (g, q, jnp.minimum((chunk(k) + 1) * (rows // halo_r), trows // halo_r - 1), 0)),
            _layer_spec((CONV_W, MXU_DIM), jr, lambda g, q, k: (0, q)),
            _layer_spec((1, MXU_DIM), jr, lambda g, q, k: (0, q)),
            _layer_spec((None, None, MXU_DIM, 2 * MXU_DIM), jr, lambda g, q, k: (phase(k), q, 0, 0)),
            _layer_spec((None, None, 1, 2 * MXU_DIM), jr, lambda g, q, k: (phase(k), q, 0, 0)),
            _layer_spec((None, 1, MXU_DIM), jr, lambda g, q, k: (phase(k), 0, q)),
            pl.BlockSpec((None, SUBLANES, MXU_DIM), lambda g, q, k: (g, 0, q)),
            pl.BlockSpec((None, SUBLANES, MXU_DIM), lambda g, q, k: (g, 0, q)),
        ],
        out_specs=[
            pl.BlockSpec((None, None, rows, LANES), lambda g, q, k: (g, q, out_chunk(k), 0)),
            pl.BlockSpec((None, SUBLANES, MXU_DIM), lambda g, q, k: (g, 0, q)),
            pl.BlockSpec((None, SUBLANES, MXU_DIM), lambda g, q, k: (g, 0, q)),
        ],
        out_shape=[
            jax.ShapeDtypeStruct((ng, nq, trows, LANES), jnp.uint32),
            jax.ShapeDtypeStruct((ng, SUBLANES, p), F32),
            jax.ShapeDtypeStruct((ng, SUBLANES, p), F32),
        ],
        scratch_shapes=[
            pltpu.VMEM((halo_l + rows + halo_r, MXU_DIM), F32),
            pltpu.VMEM((rows, MXU_DIM), F32),
            pltpu.VMEM((rows, MXU_DIM), F32),
            pltpu.VMEM((trows, MXU_DIM), F32),
            pltpu.VMEM((SUBLANES, MXU_DIM), F32),
        ],
        compiler_params=_params("parallel", "parallel", "arbitrary"),
        name="rnn_scan",
    )(xr, xr, xr, conv_w, conv_b, wg, bg, lam, h0f, h0b)


def _r3_mlp_kernel(x_ref, hs_ref, gate_ref, mod_ref, wo_ref, g_ref, w1_ref, b1_ref, w2_ref, b2_ref, *rest,
                   tt, p, tf, final):
    if final:
        fg_ref, o_ref, z_ref = rest
    else:
        o_ref, z_ref = rest
    for b in range(SUBLANES):
        for q in range(p // MXU_DIM):
            halves = _unpack_bf16_pair(hs_ref[q, pl.ds(b, tt, stride=SUBLANES), :])
            for s, hv in enumerate(halves):
                cols = slice(q * MXU_DIM + s * LANES, q * MXU_DIM + (s + 1) * LANES)
                gt = gate_ref[b, :, cols].astype(F32)
                z_ref[b * tt:(b + 1) * tt, cols] = (hv * _gelu_tanh(gt)).astype(z_ref.dtype)
    yx = _dot(z_ref[...], wo_ref[...])
    d = yx.shape[-1]
    m = mod_ref[...]
    x1 = x_ref[...] + m[:, 2:3, :] * yx.reshape(SUBLANES, tt, d)
    h = _norm_mod(x1, g_ref[...], m[:, 3:4, :], m[:, 4:5, :]).reshape(SUBLANES * tt, d).astype(_MXU_DTYPE)
    dff = w1_ref.shape[1]
    acc = jnp.zeros((SUBLANES * tt, d), F32)
    for j in range(dff // tf):
        a = _dot(h, w1_ref[:, j * tf:(j + 1) * tf]) + b1_ref[:, j * tf:(j + 1) * tf]
        a = jnp.maximum(a, 0.0)
        a = a * a
        acc = acc + _dot(a.astype(_MXU_DTYPE), w2_ref[j * tf:(j + 1) * tf, :])
    out = x1 + m[:, 5:6, :] * (acc + b2_ref[...]).reshape(SUBLANES, tt, d)
    if final:
        ms = jnp.mean(out * out, axis=-1, keepdims=True)
        out = out * lax.rsqrt(ms + EPS) * fg_ref[...]
    o_ref[...] = out


def _rnn_out_mlp(xa, hs, gate, mod, layer, wout_p, norm_g, w1, b1, w2, b2, final_g=None):
    bsz, t, d = xa.shape
    mod_arr, mod_spec = mod.per_group()
    p = gate.shape[-1]
    dff = w1.shape[-1]
    ng = bsz // SUBLANES
    nq = p // MXU_DIM
    tt = min(t, 64)
    tf = min(dff, 1024)
    final = final_g is not None
    zero2 = lambda gi, i: (0, 0)
    in_specs = [
        pl.BlockSpec((SUBLANES, tt, d), lambda gi, i: (gi, i, 0)),
        pl.BlockSpec((None, nq, tt * SUBLANES, LANES), lambda gi, i: (gi, 0, i, 0)),
        pl.BlockSpec((SUBLANES, tt, p), lambda gi, i: (gi, i, 0)),
        mod_spec,
        _layer_spec((p, d), layer // 2, zero2, resident=True),
        _layer_spec((1, d), 2 * layer + 1, zero2),
        _layer_spec((d, dff), layer, zero2, resident=True),
        _layer_spec((1, dff), layer, zero2),
        _layer_spec((dff, d), layer, zero2, resident=True),
        _layer_spec((1, d), layer, zero2),
    ]
    args = [xa, hs, gate, mod_arr, wout_p, norm_g, w1, b1, w2, b2]
    if final:
        in_specs.append(pl.BlockSpec((1, d), zero2))
        args.append(final_g.reshape(1, d))
    return pl.pallas_call(
        functools.partial(_r3_mlp_kernel, tt=tt, p=p, tf=tf, final=final),
        grid=(ng, t // tt),
        in_specs=in_specs,
        out_specs=pl.BlockSpec((SUBLANES, tt, d), lambda gi, i: (gi, i, 0)),
        out_shape=jax.ShapeDtypeStruct(xa.shape, F32),
        scratch_shapes=[pltpu.VMEM((SUBLANES * tt, p), _MXU_DTYPE)],
        compiler_params=_params("parallel", "parallel"),
        name="rnn_out_mlp",
    )(*args)


def _rnn_weights(lay, w_in, conv_w, conv_b, w_a, b_a, w_i, b_i, lam, w_out):
    nr = w_in.shape[0]
    d_rnn = conv_b.shape[-1]
    win_p = jnp.concatenate([lay.pad_last(w_in[..., :d_rnn]), lay.pad_last(w_in[..., d_rnn:])], axis=-1).astype(_MXU_DTYPE)
    cw_p = 0.5 * lay.pad_last(conv_w)
    cb_p = 0.5 * lay.pad_last(conv_b).reshape(nr, 1, lay.p)
    wg = jnp.concatenate([lay.block_diag(w_a), lay.block_diag(w_i)], axis=-1).astype(_MXU_DTYPE)
    bg = 0.5 * jnp.concatenate([lay.pad_last(b_a).reshape(nr, 2, lay.n_groups, 1, MXU_DIM),
                                lay.pad_last(b_i).reshape(nr, 2, lay.n_groups, 1, MXU_DIM)], axis=-1)
    lam_p = lay.pad_last(lam).reshape(nr, 2, 1, lay.p)
    wout_p = jnp.swapaxes(lay.pad_last(jnp.swapaxes(w_out, 1, 2)), 1, 2).astype(_MXU_DTYPE)
    return win_p, cw_p, cb_p, wg, bg, lam_p, wout_p


def kernel(x, c, ctx, c_ctx, w_mod, b_mod, norm_g, w_fourier, w_rnn_in, conv_w, conv_b,
           w_a, b_a, w_i, b_i, lam, w_rnn_out, w1, b1, w2, b2, final_g):
    bsz, seq, d = x.shape
    depth = w_mod.shape[0]
    assert bsz % SUBLANES == 0 and d % (N_FOURIER_GROUPS * LANES) == 0
    lay = _RnnLayout(w_a.shape[2], w_a.shape[-1])
    ng = bsz // SUBLANES

    pad_rows = (-(bsz + 1)) % SUBLANES
    cv = jnp.concatenate([c, c_ctx[None, :], jnp.zeros((pad_rows, d), F32)], axis=0)
    mod = _modulation(cv, w_mod, b_mod).reshape(depth, bsz + 1 + pad_rows, N_MOD, d)
    mod_c8 = jnp.broadcast_to(mod[:, bsz:bsz + 1], (depth, SUBLANES, N_MOD, d))

    w1b, w2b = w1.astype(_MXU_DTYPE), w2.astype(_MXU_DTYPE)
    wfb = w_fourier.astype(_MXU_DTYPE)
    ng2 = norm_g.reshape(depth * 2, 1, d)
    b1r, b2r = b1.reshape(depth, 1, -1), b2.reshape(depth, 1, d)
    win_p, cw_p, cb_p, wg, bg, lam_p, wout_p = _rnn_weights(
        lay, w_rnn_in, conv_w, conv_b, w_a, b_a, w_i, b_i, lam, w_rnn_out)
    zeros = jnp.zeros((ng, SUBLANES, lay.p), F32)

    for i in range(depth):
        last = i == depth - 1
        mx = _Mod(mod, mod_c8, i, False, bsz)
        mc = _Mod(mod, mod_c8, i, True, bsz)
        if i % 2 == 0:
            x = _fourier_mix_residual(x, mx, i, ng2, wfb)
            if not last:
                ctx = _fourier_mix_residual(ctx, mc, i, ng2, wfb)
        else:
            jr = i // 2
            gate_c, xr_c = _rnn_in(ctx, mc, i, ng2, win_p, lay.p)
            hs_c, hf_c, hb_c = _rnn_scan(xr_c, jr, cw_p, cb_p, wg, bg, lam_p, zeros, zeros, ctx.shape[1])
            gate_x, xr_x = _rnn_in(x, mx, i, ng2, win_p, lay.p)
            hs_x, _, _ = _rnn_scan(xr_x, jr, cw_p, cb_p, wg, bg, lam_p, hf_c, hb_c, seq)
            x = _rnn_out_mlp(x, hs_x, gate_x, mx, i, wout_p, ng2, w1b, b1r, w2b, b2r, final_g if last else None)
            if not last:
                ctx = _rnn_out_mlp(ctx, hs_c, gate_c, mc, i, wout_p, ng2, w1b, b1r, w2b, b2r)
            continue
        x = _mlp(x, mx, i, ng2, w1b, b1r, w2b, b2r, final_g if last else None)
        if not last:
            ctx = _mlp(ctx, mc, i, ng2, w1b, b1r, w2b, b2r)
    return x
```

```python
import functools
import math

import numpy as np
import jax
import jax.numpy as jnp
from jax import lax
from jax.experimental import pallas as pl
from jax.experimental.pallas import tpu as pltpu

EPS = 1e-6
RG_C = 8.0
N_FOURIER_GROUPS = 4
CONV_W = 4
N_MOD = 6

LANES = 128
SUBLANES = 8
MXU_DIM = 256
VMEM_LIMIT = 56 << 20
_MXU_DTYPE = jnp.bfloat16
_STORE_DTYPE = jnp.bfloat16
F32 = jnp.float32


def _params(*sem):
    return pltpu.CompilerParams(dimension_semantics=sem, vmem_limit_bytes=VMEM_LIMIT)


def _resident(shape, index_map):
    return pl.BlockSpec(shape, index_map, pipeline_mode=pl.Buffered(1))


def _dot(a, b):
    return jnp.dot(a, b, preferred_element_type=F32)


def _layer_spec(block, layer, index_map, resident=False):
    make = _resident if resident else pl.BlockSpec
    return make((None,) + tuple(block), lambda *ids: (layer,) + tuple(index_map(*ids)))


class _Mod:
    def __init__(self, table, table_ctx8, layer, is_ctx, bsz):
        self.table, self.table_ctx8, self.layer, self.is_ctx, self.bsz = table, table_ctx8, layer, is_ctx, bsz

    def per_batch(self):
        d = self.table.shape[-1]
        layer, fixed = self.layer, (self.bsz if self.is_ctx else None)
        return self.table, pl.BlockSpec(
            (None, None, N_MOD, d), lambda b, i: (layer, b if fixed is None else fixed, 0, 0))

    def per_group(self):
        d = self.table.shape[-1]
        layer = self.layer
        if self.is_ctx:
            return self.table_ctx8, pl.BlockSpec((None, SUBLANES, N_MOD, d), lambda gi, i: (layer, 0, 0, 0))
        return self.table, pl.BlockSpec((None, SUBLANES, N_MOD, d), lambda gi, i: (layer, gi, 0, 0))


def _norm_mod(x, g, shift, scale):
    ms = jnp.mean(x * x, axis=-1, keepdims=True)
    y = x * lax.rsqrt(ms + EPS) * g
    return y * (1.0 + scale) + shift


def _mod_kernel(cv_ref, w_ref, b_ref, o_ref):
    c = cv_ref[...]
    s = c * jax.nn.sigmoid(c)
    o_ref[...] = _dot(s.astype(_MXU_DTYPE), w_ref[...].astype(_MXU_DTYPE)) + b_ref[...]


def _modulation(cv, w_mod, b_mod):
    depth, d, n = w_mod.shape
    rows = cv.shape[0]
    tn = min(n, 1024)
    return pl.pallas_call(
        _mod_kernel,
        grid=(depth, n // tn),
        in_specs=[
            pl.BlockSpec((rows, d), lambda i, j: (0, 0)),
            pl.BlockSpec((None, d, tn), lambda i, j: (i, 0, j)),
            pl.BlockSpec((None, 1, tn), lambda i, j: (i, 0, j)),
        ],
        out_specs=pl.BlockSpec((None, rows, tn), lambda i, j: (i, 0, j)),
        out_shape=jax.ShapeDtypeStruct((depth, rows, n), F32),
        compiler_params=_params("parallel", "parallel"),
        name="modulation",
    )(cv, w_mod, b_mod.reshape(depth, 1, n))


MLP_ROWS = 1024


def _mlp_kernel(x_ref, mod_ref, g_ref, w1_ref, b1_ref, w2_ref, b2_ref, *rest, tf, final):
    if final:
        fg_ref, o_ref = rest
    else:
        (o_ref,) = rest
    x = x_ref[...]
    h = _norm_mod(x, g_ref[...], mod_ref[3:4, :], mod_ref[4:5, :]).astype(_MXU_DTYPE)
    dff = w1_ref.shape[1]
    acc = jnp.zeros(x.shape, F32)
    for j in range(dff // tf):
        a = _dot(h, w1_ref[:, j * tf:(j + 1) * tf]) + b1_ref[:, j * tf:(j + 1) * tf]
        a = jnp.maximum(a, 0.0)
        a = a * a
        acc = acc + _dot(a.astype(_MXU_DTYPE), w2_ref[j * tf:(j + 1) * tf, :])
    out = x + mod_ref[5:6, :] * (acc + b2_ref[...])
    if final:
        ms = jnp.mean(out * out, axis=-1, keepdims=True)
        out = out * lax.rsqrt(ms + EPS) * fg_ref[...]
    o_ref[...] = out


def _mlp(xa, mod, layer, norm_g, w1, b1, w2, b2, final_g=None):
    bsz, t, d = xa.shape
    dff = w1.shape[-1]
    tm = min(t, MLP_ROWS)
    tf = min(dff, 1024)
    final = final_g is not None
    mod_arr, mod_spec = mod.per_batch()
    zero2 = lambda b, i: (0, 0)
    in_specs = [
        pl.BlockSpec((None, tm, d), lambda b, i: (b, i, 0)),
        mod_spec,
        _layer_spec((1, d), 2 * layer + 1, zero2),
        _layer_spec((d, dff), layer, zero2, resident=True),
        _layer_spec((1, dff), layer, zero2),
        _layer_spec((dff, d), layer, zero2, resident=True),
        _layer_spec((1, d), layer, zero2),
    ]
    args = [xa, mod_arr, norm_g, w1, b1, w2, b2]
    if final:
        in_specs.append(pl.BlockSpec((1, d), zero2))
        args.append(final_g.reshape(1, d))
    return pl.pallas_call(
        functools.partial(_mlp_kernel, tf=tf, final=final),
        grid=(bsz, t // tm),
        in_specs=in_specs,
        out_specs=pl.BlockSpec((None, tm, d), lambda b, i: (b, i, 0)),
        out_shape=jax.ShapeDtypeStruct(xa.shape, F32),
        compiler_params=_params("parallel", "parallel"),
        name="mlp",
    )(*args)


def _dft_tables(t, gs):
    c = np.arange(gs, dtype=np.int64)
    ang = 2.0 * np.pi * ((c[:, None] * c[None, :]) % gs) / gs
    wg = np.concatenate([np.cos(ang), -np.sin(ang)], axis=1)
    k = np.arange(t, dtype=np.int64)
    angt = 2.0 * np.pi * ((k[:, None] * k[None, :]) % t) / t
    cs = np.stack([np.cos(angt), np.sin(angt)])
    return jnp.asarray(wg, _MXU_DTYPE), jnp.asarray(cs, _MXU_DTYPE)


def _f1_kernel(x_ref, mod_ref, g_ref, wg_ref, o_ref, *, gs):
    h = _norm_mod(x_ref[...], g_ref[...], mod_ref[0:1, :], mod_ref[1:2, :]).astype(_MXU_DTYPE)
    for q in range(h.shape[-1] // gs):
        res = _dot(h[:, q * gs:(q + 1) * gs], wg_ref[...])
        o_ref[0, :, q * gs:(q + 1) * gs] = res[:, :gs].astype(o_ref.dtype)
        o_ref[1, :, q * gs:(q + 1) * gs] = res[:, gs:].astype(o_ref.dtype)


def _f2_kernel(x_ref, g2_ref, cs_ref, mod_ref, wo_ref, o_ref, *, tm, scale):
    r0 = pl.multiple_of(pl.program_id(1) * tm, tm)
    y = _dot(cs_ref[0, pl.ds(r0, tm), :], g2_ref[0]) + _dot(cs_ref[1, pl.ds(r0, tm), :], g2_ref[1])
    yx = _dot((y * scale).astype(_MXU_DTYPE), wo_ref[...])
    o_ref[...] = x_ref[...] + mod_ref[2:3, :] * yx


def _fourier_layer(xa, mod, layer, norm_g, w_out):
    bsz, t, d = xa.shape
    mod_arr, mod_spec = mod.per_batch()
    zero2 = lambda b, i: (0, 0)
    jf = layer // 2
    gs = d // N_FOURIER_GROUPS
    tm = min(t, 512)
    wg, cs = _dft_tables(t, gs)
    g2 = pl.pallas_call(
        functools.partial(_f1_kernel, gs=gs),
        grid=(bsz, t // tm),
        in_specs=[
            pl.BlockSpec((None, tm, d), lambda b, i: (b, i, 0)),
            mod_spec,
            _layer_spec((1, d), 2 * layer, zero2),
            pl.BlockSpec((gs, 2 * gs), zero2),
        ],
        out_specs=pl.BlockSpec((None, 2, tm, d), lambda b, i: (b, 0, i, 0)),
        out_shape=jax.ShapeDtypeStruct((bsz, 2, t, d), _MXU_DTYPE),
        compiler_params=_params("parallel", "parallel"),
        name="fourier_group_dft",
    )(xa, mod_arr, norm_g, wg)
    scale = 1.0 / math.sqrt(t * gs)
    return pl.pallas_call(
        functools.partial(_f2_kernel, tm=tm, scale=scale),
        grid=(bsz, t // tm),
        in_specs=[
            pl.BlockSpec((None, tm, d), lambda b, i: (b, i, 0)),
            pl.BlockSpec((None, 2, t, d), lambda b, i: (b, 0, 0, 0)),
            _resident((2, t, t), lambda b, i: (0, 0, 0)),
            mod_spec,
            _layer_spec((d, d), jf, zero2, resident=True),
        ],
        out_specs=pl.BlockSpec((None, tm, d), lambda b, i: (b, i, 0)),
        out_shape=jax.ShapeDtypeStruct(xa.shape, F32),
        compiler_params=_params("parallel", "arbitrary"),
        name="fourier_time_dft",
    )(xa, g2, cs, mod_arr, w_out)


FFT_RADIX = 8
FFT_MIN_BLOCK = MXU_DIM


def _fft_tables(t, gs, tmn):
    n2 = t // FFT_RADIX
    k2 = np.arange(n2, dtype=np.int64)
    tabs = []
    for k1 in range(FFT_RADIX):
        k = k1 + FFT_RADIX * k2
        ang = 2.0 * np.pi * ((k[:, None] * k2[None, :]) % t) / t
        tabs.append(np.concatenate([np.cos(ang), np.sin(ang)], axis=1))
    c = np.arange(gs, dtype=np.int64)
    angc = 2.0 * np.pi * ((c[:, None] * c[None, :]) % gs) / gs
    wg = np.concatenate([np.cos(angc), -np.sin(angc)], axis=1)
    runs = tmn // FFT_RADIX
    pm = np.zeros((tmn, tmn))
    m = np.arange(runs)
    for k1 in range(FFT_RADIX):
        pm[FFT_RADIX * m + k1, k1 * runs + m] = 1.0
    return (jnp.asarray(np.stack(tabs), _MXU_DTYPE), jnp.asarray(wg, _MXU_DTYPE), jnp.asarray(pm, _MXU_DTYPE))


def _dft4(v):
    (ar, ai), (br, bi), (cr, ci), (dr, di) = v
    e0r, e0i, e1r, e1i = ar + cr, ai + ci, ar - cr, ai - ci
    f0r, f0i, f1r, f1i = br + dr, bi + di, br - dr, bi - di
    return [(e0r + f0r, e0i + f0i), (e1r + f1i, e1i - f1r), (e0r - f0r, e0i - f0i), (e1r - f1i, e1i + f1r)]


def _radix8(w):
    s = [(w[i][0] + w[i + 4][0], w[i][1] + w[i + 4][1]) for i in range(4)]
    dd = [(w[i][0] - w[i + 4][0], w[i][1] - w[i + 4][1]) for i in range(4)]
    rt = math.sqrt(0.5)
    tw = [dd[0],
          ((dd[1][0] + dd[1][1]) * rt, (dd[1][1] - dd[1][0]) * rt),
          (dd[2][1], -dd[2][0]),
          ((dd[3][1] - dd[3][0]) * rt, -(dd[3][0] + dd[3][1]) * rt)]
    even, odd = _dft4(s), _dft4(tw)
    return [even[0], odd[0], even[1], odd[1], even[2], odd[2], even[3], odd[3]]


def _fft_kernel(x_ref, xt_ref, mod_ref, g_ref, tab_ref, wg_ref, pm_ref, wo_ref, o_ref, z_ref, bb_ref,
                *, t, gs, tmn, scale):
    i = pl.program_id(1)
    n2 = t // FFT_RADIX
    d = x_ref.shape[-1]

    @pl.when(i == 0)
    def _():
        rstd = []
        for t1 in range(FFT_RADIX):
            xb = x_ref[t1 * n2:(t1 + 1) * n2, :]
            rstd.append(lax.rsqrt(jnp.mean(xb * xb, axis=-1, keepdims=True) + EPS))
        for q in range(d // gs):
            cols = slice(q * gs, (q + 1) * gs)
            w = []
            for t1 in range(FFT_RADIX):
                h = ((x_ref[t1 * n2:(t1 + 1) * n2, cols] * rstd[t1] * g_ref[:, cols]) * (1.0 + mod_ref[1:2, cols])
                     + mod_ref[0:1, cols])
                wc = _dot(h.astype(_MXU_DTYPE), wg_ref[...])
                w.append((wc[:, :gs], wc[:, gs:]))
            for k1, (re, im) in enumerate(_radix8(w)):
                bb_ref[k1, 0:n2, cols] = re.astype(bb_ref.dtype)
                bb_ref[k1, n2:2 * n2, cols] = im.astype(bb_ref.dtype)
        for k1 in range(FFT_RADIX):
            y = _dot(tab_ref[k1], bb_ref[k1])
            z_ref[k1 * n2:(k1 + 1) * n2, :] = (y * scale).astype(z_ref.dtype)

    runs = tmn // FFT_RADIX
    zs = jnp.concatenate(
        [z_ref[pl.ds(pl.multiple_of(k1 * n2 + i * runs, runs), runs), :] for k1 in range(FFT_RADIX)], axis=0)
    zn = _dot(pm_ref[...], zs).astype(_MXU_DTYPE)
    yx = _dot(zn, wo_ref[...])
    o_ref[...] = xt_ref[...] + mod_ref[2:3, :] * yx


def _fourier_layer_fft(xa, mod, layer, norm_g, w_out):
    bsz, t, d = xa.shape
    mod_arr, mod_spec = mod.per_batch()
    zero2 = lambda b, i: (0, 0)
    gs = d // N_FOURIER_GROUPS
    tmn = min(t, 512)
    tab, wg, pm = _fft_tables(t, gs, tmn)
    n2 = t // FFT_RADIX
    scale = 1.0 / math.sqrt(t * gs)
    return pl.pallas_call(
        functools.partial(_fft_kernel, t=t, gs=gs, tmn=tmn, scale=scale),
        grid=(bsz, t // tmn),
        in_specs=[
            pl.BlockSpec((None, t, d), lambda b, i: (jnp.minimum(b + jnp.minimum(i, 1), bsz - 1), 0, 0)),
            pl.BlockSpec((None, tmn, d), lambda b, i: (b, i, 0)),
            mod_spec,
            _layer_spec((1, d), 2 * layer, zero2),
            _resident((FFT_RADIX, n2, 2 * n2), lambda b, i: (0, 0, 0)),
            _resident((gs, 2 * gs), zero2),
            _resident((tmn, tmn), zero2),
            _layer_spec((d, d), layer // 2, zero2, resident=True),
        ],
        out_specs=pl.BlockSpec((None, tmn, d), lambda b, i: (b, i, 0)),
        out_shape=jax.ShapeDtypeStruct(xa.shape, F32),
        scratch_shapes=[pltpu.VMEM((t, d), _MXU_DTYPE), pltpu.VMEM((FFT_RADIX, 2 * n2, d), _MXU_DTYPE)],
        compiler_params=_params("parallel", "arbitrary"),
        name="fourier_fft",
    )(xa, xa, mod_arr, norm_g, tab, wg, pm, w_out)


def _fourier_mix_residual(xa, mod, layer, norm_g, w_out):
    t = xa.shape[1]
    n2 = t // FFT_RADIX
    if t % FFT_RADIX == 0 and n2 >= FFT_MIN_BLOCK and n2 % 16 == 0:
        return _fourier_layer_fft(xa, mod, layer, norm_g, w_out)
    return _fourier_layer(xa, mod, layer, norm_g, w_out)


class _RnnLayout:
    def __init__(self, n_blocks, rb):
        self.rb = rb
        self.n_blocks = n_blocks
        self.per_group = MXU_DIM // rb
        self.n_groups = -(-n_blocks // self.per_group)
        self.p = self.n_groups * MXU_DIM

    def pad_last(self, v):
        pg, rb = self.per_group, self.rb
        lead = v.shape[:-1]
        nl = len(lead)
        vb = v.reshape(lead + (self.n_blocks, rb))
        vb = jnp.pad(vb, [(0, 0)] * nl + [(0, self.n_groups * pg - self.n_blocks), (0, 0)])
        vg = vb.reshape(lead + (self.n_groups, pg * rb))
        vg = jnp.pad(vg, [(0, 0)] * nl + [(0, 0), (0, MXU_DIM - pg * rb)])
        return vg.reshape(lead + (self.p,))

    def block_diag(self, w):
        pg, rb = self.per_group, self.rb
        lead = w.shape[:-3]
        nl = len(lead)
        tot = self.n_groups * pg
        wp = jnp.pad(w, [(0, 0)] * nl + [(0, tot - self.n_blocks), (0, 0), (0, 0)])
        wp = wp.reshape(lead + (self.n_groups, pg, rb, 1, rb))
        eye = jnp.eye(pg, dtype=w.dtype).reshape(pg, 1, pg, 1)
        m = (wp * eye).reshape(lead + (self.n_groups, pg * rb, pg * rb))
        extra = MXU_DIM - pg * rb
        return jnp.pad(m, [(0, 0)] * (nl + 1) + [(0, extra), (0, extra)])


def _gelu_tanh(x):
    return 0.5 * x * (1.0 + jnp.tanh(math.sqrt(2.0 / math.pi) * (x + 0.044715 * (x * x * x))))


def _r1_kernel(x_ref, mod_ref, g_ref, win_ref, gate_ref, xr_ref, *, tt, p):
    x = x_ref[...]
    m = mod_ref[...]
    h = _norm_mod(x, g_ref[...], m[:, 0:1, :], m[:, 1:2, :])
    hb = h.reshape(SUBLANES * tt, h.shape[-1]).astype(_MXU_DTYPE)
    u = _dot(hb, win_ref[...])
    gate_ref[...] = u[:, :p].reshape(SUBLANES, tt, p).astype(gate_ref.dtype)
    for b in range(SUBLANES):
        for j in range(p // LANES):
            xr_ref[j, pl.ds(b, tt, stride=SUBLANES), :] = u[b * tt:(b + 1) * tt, p + j * LANES:p + (j + 1) * LANES]


def _rnn_in(xa, mod, layer, norm_g, win_p, p):
    bsz, t, d = xa.shape
    ng = bsz // SUBLANES
    tt = min(t, 64)
    ns = p // LANES
    mod_arr, mod_spec = mod.per_group()
    zero2 = lambda gi, i: (0, 0)
    return pl.pallas_call(
        functools.partial(_r1_kernel, tt=tt, p=p),
        grid=(ng, t // tt),
        in_specs=[
            pl.BlockSpec((SUBLANES, tt, d), lambda gi, i: (gi, i, 0)),
            mod_spec,
            _layer_spec((1, d), 2 * layer, zero2),
            _layer_spec((d, 2 * p), layer // 2, zero2, resident=True),
        ],
        out_specs=[
            pl.BlockSpec((SUBLANES, tt, p), lambda gi, i: (gi, i, 0)),
            pl.BlockSpec((None, ns, tt * SUBLANES, LANES), lambda gi, i: (gi, 0, i, 0)),
        ],
        out_shape=[
            jax.ShapeDtypeStruct((bsz, t, p), _STORE_DTYPE),
            jax.ShapeDtypeStruct((ng, ns, t * SUBLANES, LANES), F32),
        ],
        compiler_params=_params("parallel", "parallel"),
        name="rnn_in_proj",
    )(xa, mod_arr, norm_g, win_p)


SCAN_CHUNK = 256


def _softplus(z):
    return jnp.maximum(z, 0.0) + jnp.log(1.0 + jnp.exp(-jnp.abs(z)))


def _pack_bf16_pair(hi, lo):
    hb = lax.bitcast_convert_type(hi.astype(jnp.bfloat16).astype(F32), jnp.uint32)
    lb = lax.bitcast_convert_type(lo.astype(jnp.bfloat16).astype(F32), jnp.uint32)
    return hb | (lb >> 16)


def _unpack_bf16_pair(w):
    hi = lax.bitcast_convert_type(w & jnp.uint32(0xFFFF0000), F32)
    lo = lax.bitcast_convert_type(w << 16, F32)
    return hi, lo


def _r2_kernel(xm_ref, xp_ref, xn_ref, cw_ref, cb_ref, wg_ref, bg_ref, lam_ref, h0f_ref, h0b_ref,
               hs_ref, hff_ref, hfb_ref, ext_ref, a_ref, b_ref, hfall_ref, xh_ref, st_ref, *, tt, n_t):
    k = pl.program_id(2)
    rows = tt * SUBLANES
    halo_l = (CONV_W // 2) * SUBLANES
    halo_r = (CONV_W - 1 - CONV_W // 2) * SUBLANES
    gw = MXU_DIM

    def compute(fwd_job):
        c = jnp.where(k < n_t, k, 2 * n_t - 1 - k)
        crow = pl.ds(pl.multiple_of(c * rows, rows), rows)
        if fwd_job:
            for s in range(gw // LANES):
                ls = slice(s * LANES, (s + 1) * LANES)
                ext_ref[0:halo_l, ls] = jnp.where(c > 0, xp_ref[s], 0.0)
                ext_ref[halo_l:halo_l + rows, ls] = xm_ref[s]
                ext_ref[halo_l + rows:halo_l + rows + halo_r, ls] = jnp.where(c < n_t - 1, xn_ref[s], 0.0)
            xh = cb_ref[...] + sum(
                cw_ref[kk:kk + 1, :] * ext_ref[kk * SUBLANES:kk * SUBLANES + rows, :] for kk in range(CONV_W))
            xh_ref[crow, :] = xh
        else:
            xh = xh_ref[crow, :]
        p = _dot(xh.astype(_MXU_DTYPE), wg_ref[...]) + bg_ref[...]
        th_r = jnp.tanh(p[:, :gw])
        th_i = jnp.tanh(p[:, gw:])
        kh = (-0.5 * RG_C * math.log2(math.e)) * _softplus(-lam_ref[...])
        a = jnp.exp2(kh * th_r + kh)
        y = 1.0 - a * a
        a_ref[...] = a
        b_ref[...] = (y * lax.rsqrt(jnp.maximum(y, 1e-30))) * ((th_i + 1.0) * xh)

    def scan(fwd):
        j = k - 1
        if fwd:
            base = j * rows
            h = jnp.where(j == 0, h0f_ref[...], st_ref[...])
        else:
            base = (2 * n_t - 1 - j) * rows
            h = jnp.where(j == n_t, h0b_ref[...], st_ref[...])
        for t in (range(tt) if fwd else reversed(range(tt))):
            o = t * SUBLANES
            h = a_ref[o:o + SUBLANES, :] * h + b_ref[o:o + SUBLANES, :]
            hrow = pl.ds(pl.multiple_of(base + o, SUBLANES), SUBLANES)
            if fwd:
                hfall_ref[hrow, :] = h
            else:
                tot = hfall_ref[hrow, :] + h
                hs_ref[o:o + SUBLANES, :] = _pack_bf16_pair(tot[:, :LANES], tot[:, LANES:])
        st_ref[...] = h

    @pl.when(k == 0)
    def _():
        compute(True)

    @pl.when(jnp.logical_and(k >= 1, k < n_t))
    def _():
        scan(True)
        compute(True)

    @pl.when(k == n_t)
    def _():
        scan(True)
        compute(False)
        hff_ref[...] = st_ref[...]

    @pl.when(jnp.logical_and(k > n_t, k < 2 * n_t))
    def _():
        scan(False)
        compute(False)

    @pl.when(k == 2 * n_t)
    def _():
        scan(False)
        hfb_ref[...] = st_ref[...]


def _rnn_scan(xr, jr, conv_w, conv_b, wg, bg, lam, h0f, h0b, t):
    ng, ns, trows, _ = xr.shape
    p = ns * LANES
    nq = p // MXU_DIM
    spg = MXU_DIM // LANES
    assert spg == 2
    tt = min(t, SCAN_CHUNK)
    n_t = t // tt
    rows = tt * SUBLANES
    halo_l = (CONV_W // 2) * SUBLANES
    halo_r = (CONV_W - 1 - CONV_W // 2) * SUBLANES

    def chunk(k):
        return jnp.minimum(k, n_t - 1)

    def out_chunk(k):
        return jnp.where(k <= n_t, n_t - 1, 2 * n_t - k)

    def phase(k):
        return jnp.where(k < n_t, 0, 1)

    return pl.pallas_call(
        functools.partial(_r2_kernel, tt=tt, n_t=n_t),
        grid=(ng, nq, 2 * n_t + 1),
        in_specs=[
            pl.BlockSpec((None, spg, rows, LANES), lambda g, q, k: (g, q, chunk(k), 0)),
            pl.BlockSpec((None, spg, halo_l, LANES),
                         lambda g, q, k: (g, q, jnp.maximum(chunk(k) * (rows // halo_l) - 1, 0), 0)),
            pl.BlockSpec((None, spg, halo_r, LANES),
                         lambda g, q, k: (g, q, jnp.minimum((chunk(k) + 1) * (rows // halo_r), trows // halo_r - 1), 0)),
            _layer_spec((CONV_W, MXU_DIM), jr, lambda g, q, k: (0, q)),
            _layer_spec((1, MXU_DIM), jr, lambda g, q, k: (0, q)),
            _layer_spec((None, None, MXU_DIM, 2 * MXU_DIM), jr, lambda g, q, k: (phase(k), q, 0, 0)),
            _layer_spec((None, None, 1, 2 * MXU_DIM), jr, lambda g, q, k: (phase(k), q, 0, 0)),
            _layer_spec((None, 1, MXU_DIM), jr, lambda g, q, k: (phase(k), 0, q)),
            pl.BlockSpec((None, SUBLANES, MXU_DIM), lambda g, q, k: (g, 0, q)),
            pl.BlockSpec((None, SUBLANES, MXU_DIM), lambda g, q, k: (g, 0, q)),
        ],
        out_specs=[
            pl.BlockSpec((None, None, rows, LANES), lambda g, q, k: (g, q, out_chunk(k), 0)),
            pl.BlockSpec((None, SUBLANES, MXU_DIM), lambda g, q, k: (g, 0, q)),
            pl.BlockSpec((None, SUBLANES, MXU_DIM), lambda g, q, k: (g, 0, q)),
        ],
        out_shape=[
            jax.ShapeDtypeStruct((ng, nq, trows, LANES), jnp.uint32),
            jax.ShapeDtypeStruct((ng, SUBLANES, p), F32),
            jax.ShapeDtypeStruct((ng, SUBLANES, p), F32),
        ],
        scratch_shapes=[
            pltpu.VMEM((halo_l + rows + halo_r, MXU_DIM), F32),
            pltpu.VMEM((rows, MXU_DIM), F32),
            pltpu.VMEM((rows, MXU_DIM), F32),
            pltpu.VMEM((trows, MXU_DIM), F32),
            pltpu.VMEM((trows, MXU_DIM), F32),
            pltpu.VMEM((SUBLANES, MXU_DIM), F32),
        ],
        compiler_params=_params("parallel", "parallel", "arbitrary"),
        name="rnn_scan",
    )(xr, xr, xr, conv_w, conv_b, wg, bg, lam, h0f, h0b)


def _r3_mlp_kernel(x_ref, hs_ref, gate_ref, mod_ref, wo_ref, g_ref, w1_ref, b1_ref, w2_ref, b2_ref, *rest,
                   tt, p, tf, final):
    if final:
        fg_ref, o_ref, z_ref = rest
    else:
        o_ref, z_ref = rest
    for b in range(SUBLANES):
        for q in range(p // MXU_DIM):
            halves = _unpack_bf16_pair(hs_ref[q, pl.ds(b, tt, stride=SUBLANES), :])
            for s, hv in enumerate(halves):
                cols = slice(q * MXU_DIM + s * LANES, q * MXU_DIM + (s + 1) * LANES)
                gt = gate_ref[b, :, cols].astype(F32)
                z_ref[b * tt:(b + 1) * tt, cols] = (hv * _gelu_tanh(gt)).astype(z_ref.dtype)
    yx = _dot(z_ref[...], wo_ref[...])
    d = yx.shape[-1]
    m = mod_ref[...]
    x1 = x_ref[...] + m[:, 2:3, :] * yx.reshape(SUBLANES, tt, d)
    h = _norm_mod(x1, g_ref[...], m[:, 3:4, :], m[:, 4:5, :]).reshape(SUBLANES * tt, d).astype(_MXU_DTYPE)
    dff = w1_ref.shape[1]
    acc = jnp.zeros((SUBLANES * tt, d), F32)
    for j in range(dff // tf):
        a = _dot(h, w1_ref[:, j * tf:(j + 1) * tf]) + b1_ref[:, j * tf:(j + 1) * tf]
        a = jnp.maximum(a, 0.0)
        a = a * a
        acc = acc + _dot(a.astype(_MXU_DTYPE), w2_ref[j * tf:(j + 1) * tf, :])
    out = x1 + m[:, 5:6, :] * (acc + b2_ref[...]).reshape(SUBLANES, tt, d)
    if final:
        ms = jnp.mean(out * out, axis=-1, keepdims=True)
        out = out * lax.rsqrt(ms + EPS) * fg_ref[...]
    o_ref[...] = out


def _rnn_out_mlp(xa, hs, gate, mod, layer, wout_p, norm_g, w1, b1, w2, b2, final_g=None):
    bsz, t, d = xa.shape
    mod_arr, mod_spec = mod.per_group()
    p = gate.shape[-1]
    dff = w1.shape[-1]
    ng = bsz // SUBLANES
    nq = p // MXU_DIM
    tt = min(t, 64)
    tf = min(dff, 1024)
    final = final_g is not None
    zero2 = lambda gi, i: (0, 0)
    in_specs = [
        pl.BlockSpec((SUBLANES, tt, d), lambda gi, i: (gi, i, 0)),
        pl.BlockSpec((None, nq, tt * SUBLANES, LANES), lambda gi, i: (gi, 0, i, 0)),
        pl.BlockSpec((SUBLANES, tt, p), lambda gi, i: (gi, i, 0)),
        mod_spec,
        _layer_spec((p, d), layer // 2, zero2, resident=True),
        _layer_spec((1, d), 2 * layer + 1, zero2),
        _layer_spec((d, dff), layer, zero2, resident=True),
        _layer_spec((1, dff), layer, zero2),
        _layer_spec((dff, d), layer, zero2, resident=True),
        _layer_spec((1, d), layer, zero2),
    ]
    args = [xa, hs, gate, mod_arr, wout_p, norm_g, w1, b1, w2, b2]
    if final:
        in_specs.append(pl.BlockSpec((1, d), zero2))
        args.append(final_g.reshape(1, d))
    return pl.pallas_call(
        functools.partial(_r3_mlp_kernel, tt=tt, p=p, tf=tf, final=final),
        grid=(ng, t // tt),
        in_specs=in_specs,
        out_specs=pl.BlockSpec((SUBLANES, tt, d), lambda gi, i: (gi, i, 0)),
        out_shape=jax.ShapeDtypeStruct(xa.shape, F32),
        scratch_shapes=[pltpu.VMEM((SUBLANES * tt, p), _MXU_DTYPE)],
        compiler_params=_params("parallel", "parallel"),
        name="rnn_out_mlp",
    )(*args)


def _rnn_weights(lay, w_in, conv_w, conv_b, w_a, b_a, w_i, b_i, lam, w_out):
    nr = w_in.shape[0]
    d_rnn = conv_b.shape[-1]
    win_p = jnp.concatenate([lay.pad_last(w_in[..., :d_rnn]), lay.pad_last(w_in[..., d_rnn:])], axis=-1).astype(_MXU_DTYPE)
    cw_p = 0.5 * lay.pad_last(conv_w)
    cb_p = 0.5 * lay.pad_last(conv_b).reshape(nr, 1, lay.p)
    wg = jnp.concatenate([lay.block_diag(w_a), lay.block_diag(w_i)], axis=-1).astype(_MXU_DTYPE)
    bg = 0.5 * jnp.concatenate([lay.pad_last(b_a).reshape(nr, 2, lay.n_groups, 1, MXU_DIM),
                                lay.pad_last(b_i).reshape(nr, 2, lay.n_groups, 1, MXU_DIM)], axis=-1)
    lam_p = lay.pad_last(lam).reshape(nr, 2, 1, lay.p)
    wout_p = jnp.swapaxes(lay.pad_last(jnp.swapaxes(w_out, 1, 2)), 1, 2).astype(_MXU_DTYPE)
    return win_p, cw_p, cb_p, wg, bg, lam_p, wout_p


def kernel(x, c, ctx, c_ctx, w_mod, b_mod, norm_g, w_fourier, w_rnn_in, conv_w, conv_b,
           w_a, b_a, w_i, b_i, lam, w_rnn_out, w1, b1, w2, b2, final_g):
    bsz, seq, d = x.shape
    depth = w_mod.shape[0]
    assert bsz % SUBLANES == 0 and d % (N_FOURIER_GROUPS * LANES) == 0
    lay = _RnnLayout(w_a.shape[2], w_a.shape[-1])
    ng = bsz // SUBLANES

    pad_rows = (-(bsz + 1)) % SUBLANES
    cv = jnp.concatenate([c, c_ctx[None, :], jnp.zeros((pad_rows, d), F32)], axis=0)
    mod = _modulation(cv, w_mod, b_mod).reshape(depth, bsz + 1 + pad_rows, N_MOD, d)
    mod_c8 = jnp.broadcast_to(mod[:, bsz:bsz + 1], (depth, SUBLANES, N_MOD, d))

    w1b, w2b = w1.astype(_MXU_DTYPE), w2.astype(_MXU_DTYPE)
    wfb = w_fourier.astype(_MXU_DTYPE)
    ng2 = norm_g.reshape(depth * 2, 1, d)
    b1r, b2r = b1.reshape(depth, 1, -1), b2.reshape(depth, 1, d)
    win_p, cw_p, cb_p, wg, bg, lam_p, wout_p = _rnn_weights(
        lay, w_rnn_in, conv_w, conv_b, w_a, b_a, w_i, b_i, lam, w_rnn_out)
    zeros = jnp.zeros((ng, SUBLANES, lay.p), F32)

    for i in range(depth):
        last = i == depth - 1
        mx = _Mod(mod, mod_c8, i, False, bsz)
        mc = _Mod(mod, mod_c8, i, True, bsz)
        if i % 2 == 0:
            x = _fourier_mix_residual(x, mx, i, ng2, wfb)
            if not last:
                ctx = _fourier_mix_residual(ctx, mc, i, ng2, wfb)
        else:
            jr = i // 2
            gate_c, xr_c = _rnn_in(ctx, mc, i, ng2, win_p, lay.p)
            hs_c, hf_c, hb_c = _rnn_scan(xr_c, jr, cw_p, cb_p, wg, bg, lam_p, zeros, zeros, ctx.shape[1])
            gate_x, xr_x = _rnn_in(x, mx, i, ng2, win_p, lay.p)
            hs_x, _, _ = _rnn_scan(xr_x, jr, cw_p, cb_p, wg, bg, lam_p, hf_c, hb_c, seq)
            x = _rnn_out_mlp(x, hs_x, gate_x, mx, i, wout_p, ng2, w1b, b1r, w2b, b2r, final_g if last else None)
            if not last:
                ctx = _rnn_out_mlp(ctx, hs_c, gate_c, mc, i, wout_p, ng2, w1b, b1r, w2b, b2r)
            continue
        x = _mlp(x, mx, i, ng2, w1b, b1r, w2b, b2r, final_g if last else None)
        if not last:
            ctx = _mlp(ctx, mc, i, ng2, w1b, b1r, w2b, b2r)
    return x
```

```python
import functools
import math

import numpy as np
import jax
import jax.numpy as jnp
from jax import lax
from jax.experimental import pallas as pl
from jax.experimental.pallas import tpu as pltpu

EPS = 1e-6
RG_C = 8.0
N_FOURIER_GROUPS = 4
CONV_W = 4
N_MOD = 6

LANES = 128
SUBLANES = 8
MXU_DIM = 256
VMEM_LIMIT = 56 << 20
_MXU_DTYPE = jnp.bfloat16
_STORE_DTYPE = jnp.bfloat16
F32 = jnp.float32


def _params(*sem):
    return pltpu.CompilerParams(dimension_semantics=sem, vmem_limit_bytes=VMEM_LIMIT)


def _resident(shape, index_map):
    return pl.BlockSpec(shape, index_map, pipeline_mode=pl.Buffered(1))


def _dot(a, b):
    return jnp.dot(a, b, preferred_element_type=F32)


def _layer_spec(block, layer, index_map, resident=False):
    make = _resident if resident else pl.BlockSpec
    return make((None,) + tuple(block), lambda *ids: (layer,) + tuple(index_map(*ids)))


class _Mod:
    def __init__(self, table, table_ctx8, layer, is_ctx, bsz):
        self.table, self.table_ctx8, self.layer, self.is_ctx, self.bsz = table, table_ctx8, layer, is_ctx, bsz

    def per_batch(self):
        d = self.table.shape[-1]
        layer, fixed = self.layer, (self.bsz if self.is_ctx else None)
        return self.table, pl.BlockSpec(
            (None, None, N_MOD, d), lambda b, i: (layer, b if fixed is None else fixed, 0, 0))

    def per_group(self):
        d = self.table.shape[-1]
        layer = self.layer
        if self.is_ctx:
            return self.table_ctx8, pl.BlockSpec((None, SUBLANES, N_MOD, d), lambda gi, i: (layer, 0, 0, 0))
        return self.table, pl.BlockSpec((None, SUBLANES, N_MOD, d), lambda gi, i: (layer, gi, 0, 0))


def _norm_mod(x, g, shift, scale):
    ms = jnp.mean(x * x, axis=-1, keepdims=True)
    y = x * lax.rsqrt(ms + EPS) * g
    return y * (1.0 + scale) + shift


def _mod_kernel(cv_ref, w_ref, b_ref, o_ref):
    c = cv_ref[...]
    s = c * jax.nn.sigmoid(c)
    o_ref[...] = _dot(s.astype(_MXU_DTYPE), w_ref[...].astype(_MXU_DTYPE)) + b_ref[...]


def _modulation(cv, w_mod, b_mod):
    depth, d, n = w_mod.shape
    rows = cv.shape[0]
    tn = min(n, 1024)
    return pl.pallas_call(
        _mod_kernel,
        grid=(depth, n // tn),
        in_specs=[
            pl.BlockSpec((rows, d), lambda i, j: (0, 0)),
            pl.BlockSpec((None, d, tn), lambda i, j: (i, 0, j)),
            pl.BlockSpec((None, 1, tn), lambda i, j: (i, 0, j)),
        ],
        out_specs=pl.BlockSpec((None, rows, tn), lambda i, j: (i, 0, j)),
        out_shape=jax.ShapeDtypeStruct((depth, rows, n), F32),
        compiler_params=_params("parallel", "parallel"),
        name="modulation",
    )(cv, w_mod, b_mod.reshape(depth, 1, n))


MLP_ROWS = 1024


def _mlp_kernel(x_ref, mod_ref, g_ref, w1_ref, b1_ref, w2_ref, b2_ref, *rest, tf, final):
    if final:
        fg_ref, o_ref = rest
    else:
        (o_ref,) = rest
    x = x_ref[...]
    h = _norm_mod(x, g_ref[...], mod_ref[3:4, :], mod_ref[4:5, :]).astype(_MXU_DTYPE)
    dff = w1_ref.shape[1]
    acc = jnp.zeros(x.shape, F32)
    for j in range(dff // tf):
        a = _dot(h, w1_ref[:, j * tf:(j + 1) * tf]) + b1_ref[:, j * tf:(j + 1) * tf]
        a = jnp.maximum(a, 0.0)
        a = a * a
        acc = acc + _dot(a.astype(_MXU_DTYPE), w2_ref[j * tf:(j + 1) * tf, :])
    out = x + mod_ref[5:6, :] * (acc + b2_ref[...])
    if final:
        ms = jnp.mean(out * out, axis=-1, keepdims=True)
        out = out * lax.rsqrt(ms + EPS) * fg_ref[...]
    o_ref[...] = out


def _mlp(xa, mod, layer, norm_g, w1, b1, w2, b2, final_g=None):
    bsz, t, d = xa.shape
    dff = w1.shape[-1]
    tm = min(t, MLP_ROWS)
    tf = min(dff, 1024)
    final = final_g is not None
    mod_arr, mod_spec = mod.per_batch()
    zero2 = lambda b, i: (0, 0)
    in_specs = [
        pl.BlockSpec((None, tm, d), lambda b, i: (b, i, 0)),
        mod_spec,
        _layer_spec((1, d), 2 * layer + 1, zero2),
        _layer_spec((d, dff), layer, zero2, resident=True),
        _layer_spec((1, dff), layer, zero2),
        _layer_spec((dff, d), layer, zero2, resident=True),
        _layer_spec((1, d), layer, zero2),
    ]
    args = [xa, mod_arr, norm_g, w1, b1, w2, b2]
    if final:
        in_specs.append(pl.BlockSpec((1, d), zero2))
        args.append(final_g.reshape(1, d))
    return pl.pallas_call(
        functools.partial(_mlp_kernel, tf=tf, final=final),
        grid=(bsz, t // tm),
        in_specs=in_specs,
        out_specs=pl.BlockSpec((None, tm, d), lambda b, i: (b, i, 0)),
        out_shape=jax.ShapeDtypeStruct(xa.shape, F32),
        compiler_params=_params("parallel", "parallel"),
        name="mlp",
    )(*args)


def _dft_tables(t, gs):
    c = np.arange(gs, dtype=np.int64)
    ang = 2.0 * np.pi * ((c[:, None] * c[None, :]) % gs) / gs
    wg = np.concatenate([np.cos(ang), -np.sin(ang)], axis=1)
    k = np.arange(t, dtype=np.int64)
    angt = 2.0 * np.pi * ((k[:, None] * k[None, :]) % t) / t
    cs = np.stack([np.cos(angt), np.sin(angt)])
    return jnp.asarray(wg, _MXU_DTYPE), jnp.asarray(cs, _MXU_DTYPE)


def _f1_kernel(x_ref, mod_ref, g_ref, wg_ref, o_ref, *, gs):
    h = _norm_mod(x_ref[...], g_ref[...], mod_ref[0:1, :], mod_ref[1:2, :]).astype(_MXU_DTYPE)
    for q in range(h.shape[-1] // gs):
        res = _dot(h[:, q * gs:(q + 1) * gs], wg_ref[...])
        o_ref[0, :, q * gs:(q + 1) * gs] = res[:, :gs].astype(o_ref.dtype)
        o_ref[1, :, q * gs:(q + 1) * gs] = res[:, gs:].astype(o_ref.dtype)


def _f2_kernel(x_ref, g2_ref, cs_ref, mod_ref, wo_ref, o_ref, *, tm, scale):
    r0 = pl.multiple_of(pl.program_id(1) * tm, tm)
    y = _dot(cs_ref[0, pl.ds(r0, tm), :], g2_ref[0]) + _dot(cs_ref[1, pl.ds(r0, tm), :], g2_ref[1])
    yx = _dot((y * scale).astype(_MXU_DTYPE), wo_ref[...])
    o_ref[...] = x_ref[...] + mod_ref[2:3, :] * yx


def _fourier_layer(xa, mod, layer, norm_g, w_out):
    bsz, t, d = xa.shape
    mod_arr, mod_spec = mod.per_batch()
    zero2 = lambda b, i: (0, 0)
    jf = layer // 2
    gs = d // N_FOURIER_GROUPS
    tm = min(t, 512)
    wg, cs = _dft_tables(t, gs)
    g2 = pl.pallas_call(
        functools.partial(_f1_kernel, gs=gs),
        grid=(bsz, t // tm),
        in_specs=[
            pl.BlockSpec((None, tm, d), lambda b, i: (b, i, 0)),
            mod_spec,
            _layer_spec((1, d), 2 * layer, zero2),
            pl.BlockSpec((gs, 2 * gs), zero2),
        ],
        out_specs=pl.BlockSpec((None, 2, tm, d), lambda b, i: (b, 0, i, 0)),
        out_shape=jax.ShapeDtypeStruct((bsz, 2, t, d), _MXU_DTYPE),
        compiler_params=_params("parallel", "parallel"),
        name="fourier_group_dft",
    )(xa, mod_arr, norm_g, wg)
    scale = 1.0 / math.sqrt(t * gs)
    return pl.pallas_call(
        functools.partial(_f2_kernel, tm=tm, scale=scale),
        grid=(bsz, t // tm),
        in_specs=[
            pl.BlockSpec((None, tm, d), lambda b, i: (b, i, 0)),
            pl.BlockSpec((None, 2, t, d), lambda b, i: (b, 0, 0, 0)),
            _resident((2, t, t), lambda b, i: (0, 0, 0)),
            mod_spec,
            _layer_spec((d, d), jf, zero2, resident=True),
        ],
        out_specs=pl.BlockSpec((None, tm, d), lambda b, i: (b, i, 0)),
        out_shape=jax.ShapeDtypeStruct(xa.shape, F32),
        compiler_params=_params("parallel", "arbitrary"),
        name="fourier_time_dft",
    )(xa, g2, cs, mod_arr, w_out)


FFT_RADIX = 8
FFT_MIN_BLOCK = MXU_DIM


def _fft_tables(t, gs, tmn):
    n2 = t // FFT_RADIX
    k2 = np.arange(n2, dtype=np.int64)
    tabs = []
    for k1 in range(FFT_RADIX):
        k = k1 + FFT_RADIX * k2
        ang = 2.0 * np.pi * ((k[:, None] * k2[None, :]) % t) / t
        tabs.append(np.concatenate([np.cos(ang), np.sin(ang)], axis=1))
    c = np.arange(gs, dtype=np.int64)
    angc = 2.0 * np.pi * ((c[:, None] * c[None, :]) % gs) / gs
    wg = np.concatenate([np.cos(angc), -np.sin(angc)], axis=1)
    runs = tmn // FFT_RADIX
    pm = np.zeros((tmn, tmn))
    m = np.arange(runs)
    for k1 in range(FFT_RADIX):
        pm[FFT_RADIX * m + k1, k1 * runs + m] = 1.0
    return (jnp.asarray(np.stack(tabs), _MXU_DTYPE), jnp.asarray(wg, _MXU_DTYPE), jnp.asarray(pm, _MXU_DTYPE))


def _dft4(v):
    (ar, ai), (br, bi), (cr, ci), (dr, di) = v
    e0r, e0i, e1r, e1i = ar + cr, ai + ci, ar - cr, ai - ci
    f0r, f0i, f1r, f1i = br + dr, bi + di, br - dr, bi - di
    return [(e0r + f0r, e0i + f0i), (e1r + f1i, e1i - f1r), (e0r - f0r, e0i - f0i), (e1r - f1i, e1i + f1r)]


def _radix8(w):
    s = [(w[i][0] + w[i + 4][0], w[i][1] + w[i + 4][1]) for i in range(4)]
    dd = [(w[i][0] - w[i + 4][0], w[i][1] - w[i + 4][1]) for i in range(4)]
    rt = math.sqrt(0.5)
    tw = [dd[0],
          ((dd[1][0] + dd[1][1]) * rt, (dd[1][1] - dd[1][0]) * rt),
          (dd[2][1], -dd[2][0]),
          ((dd[3][1] - dd[3][0]) * rt, -(dd[3][0] + dd[3][1]) * rt)]
    even, odd = _dft4(s), _dft4(tw)
    return [even[0], odd[0], even[1], odd[1], even[2], odd[2], even[3], odd[3]]


def _fft_kernel(x_ref, xt_ref, mod_ref, g_ref, tab_ref, wg_ref, pm_ref, wo_ref, o_ref, z_ref, bb_ref,
                *, t, gs, tmn, scale):
    i = pl.program_id(1)
    n2 = t // FFT_RADIX
    d = x_ref.shape[-1]

    @pl.when(i == 0)
    def _():
        rstd = []
        for t1 in range(FFT_RADIX):
            xb = x_ref[t1 * n2:(t1 + 1) * n2, :]
            rstd.append(lax.rsqrt(jnp.mean(xb * xb, axis=-1, keepdims=True) + EPS))
        for q in range(d // gs):
            cols = slice(q * gs, (q + 1) * gs)
            w = []
            for t1 in range(FFT_RADIX):
                h = ((x_ref[t1 * n2:(t1 + 1) * n2, cols] * rstd[t1] * g_ref[:, cols]) * (1.0 + mod_ref[1:2, cols])
                     + mod_ref[0:1, cols])
                wc = _dot(h.astype(_MXU_DTYPE), wg_ref[...])
                w.append((wc[:, :gs], wc[:, gs:]))
            for k1, (re, im) in enumerate(_radix8(w)):
                bb_ref[k1, 0:n2, cols] = re.astype(bb_ref.dtype)
                bb_ref[k1, n2:2 * n2, cols] = im.astype(bb_ref.dtype)
        for k1 in range(FFT_RADIX):
            y = _dot(tab_ref[k1], bb_ref[k1])
            z_ref[k1 * n2:(k1 + 1) * n2, :] = (y * scale).astype(z_ref.dtype)

    runs = tmn // FFT_RADIX
    zs = jnp.concatenate(
        [z_ref[pl.ds(pl.multiple_of(k1 * n2 + i * runs, runs), runs), :] for k1 in range(FFT_RADIX)], axis=0)
    zn = _dot(pm_ref[...], zs).astype(_MXU_DTYPE)
    yx = _dot(zn, wo_ref[...])
    o_ref[...] = xt_ref[...] + mod_ref[2:3, :] * yx


def _fourier_layer_fft(xa, mod, layer, norm_g, w_out):
    bsz, t, d = xa.shape
    mod_arr, mod_spec = mod.per_batch()
    zero2 = lambda b, i: (0, 0)
    gs = d // N_FOURIER_GROUPS
    tmn = min(t, 512)
    tab, wg, pm = _fft_tables(t, gs, tmn)
    n2 = t // FFT_RADIX
    scale = 1.0 / math.sqrt(t * gs)
    return pl.pallas_call(
        functools.partial(_fft_kernel, t=t, gs=gs, tmn=tmn, scale=scale),
        grid=(bsz, t // tmn),
        in_specs=[
            pl.BlockSpec((None, t, d), lambda b, i: (jnp.minimum(b + jnp.minimum(i, 1), bsz - 1), 0, 0)),
            pl.BlockSpec((None, tmn, d), lambda b, i: (b, i, 0)),
            mod_spec,
            _layer_spec((1, d), 2 * layer, zero2),
            _resident((FFT_RADIX, n2, 2 * n2), lambda b, i: (0, 0, 0)),
            _resident((gs, 2 * gs), zero2),
            _resident((tmn, tmn), zero2),
            _layer_spec((d, d), layer // 2, zero2, resident=True),
        ],
        out_specs=pl.BlockSpec((None, tmn, d), lambda b, i: (b, i, 0)),
        out_shape=jax.ShapeDtypeStruct(xa.shape, F32),
        scratch_shapes=[pltpu.VMEM((t, d), _MXU_DTYPE), pltpu.VMEM((FFT_RADIX, 2 * n2, d), _MXU_DTYPE)],
        compiler_params=_params("parallel", "arbitrary"),
        name="fourier_fft",
    )(xa, xa, mod_arr, norm_g, tab, wg, pm, w_out)


def _fourier_mix_residual(xa, mod, layer, norm_g, w_out):
    t = xa.shape[1]
    n2 = t // FFT_RADIX
    if t % FFT_RADIX == 0 and n2 >= FFT_MIN_BLOCK and n2 % 16 == 0:
        return _fourier_layer_fft(xa, mod, layer, norm_g, w_out)
    return _fourier_layer(xa, mod, layer, norm_g, w_out)


class _RnnLayout:
    def __init__(self, n_blocks, rb):
        self.rb = rb
        self.n_blocks = n_blocks
        self.per_group = MXU_DIM // rb
        self.n_groups = -(-n_blocks // self.per_group)
        self.p = self.n_groups * MXU_DIM

    def pad_last(self, v):
        pg, rb = self.per_group, self.rb
        lead = v.shape[:-1]
        nl = len(lead)
        vb = v.reshape(lead + (self.n_blocks, rb))
        vb = jnp.pad(vb, [(0, 0)] * nl + [(0, self.n_groups * pg - self.n_blocks), (0, 0)])
        vg = vb.reshape(lead + (self.n_groups, pg * rb))
        vg = jnp.pad(vg, [(0, 0)] * nl + [(0, 0), (0, MXU_DIM - pg * rb)])
        return vg.reshape(lead + (self.p,))

    def pad_rows(self, w):
        pg, rb = self.per_group, self.rb
        lead, d = w.shape[:-2], w.shape[-1]
        nl = len(lead)
        wb = w.reshape(lead + (self.n_blocks, rb, d))
        wb = jnp.pad(wb, [(0, 0)] * nl + [(0, self.n_groups * pg - self.n_blocks), (0, 0), (0, 0)])
        wg = wb.reshape(lead + (self.n_groups, pg * rb, d))
        wg = jnp.pad(wg, [(0, 0)] * nl + [(0, 0), (0, MXU_DIM - pg * rb), (0, 0)])
        return wg.reshape(lead + (self.p, d))

    def block_diag(self, w):
        pg, rb = self.per_group, self.rb
        lead = w.shape[:-3]
        nl = len(lead)
        tot = self.n_groups * pg
        wp = jnp.pad(w, [(0, 0)] * nl + [(0, tot - self.n_blocks), (0, 0), (0, 0)])
        wp = wp.reshape(lead + (self.n_groups, pg, rb, 1, rb))
        eye = jnp.eye(pg, dtype=w.dtype).reshape(pg, 1, pg, 1)
        m = (wp * eye).reshape(lead + (self.n_groups, pg * rb, pg * rb))
        extra = MXU_DIM - pg * rb
        return jnp.pad(m, [(0, 0)] * (nl + 1) + [(0, extra), (0, extra)])


def _gelu_tanh(x):
    return 0.5 * x * (1.0 + jnp.tanh(math.sqrt(2.0 / math.pi) * (x + 0.044715 * (x * x * x))))


def _r1_kernel(x_ref, mod_ref, g_ref, win_ref, gate_ref, xr_ref, *, tt, p):
    x = x_ref[...]
    m = mod_ref[...]
    h = _norm_mod(x, g_ref[...], m[:, 0:1, :], m[:, 1:2, :])
    hb = h.reshape(SUBLANES * tt, h.shape[-1]).astype(_MXU_DTYPE)
    u = _dot(hb, win_ref[...])
    gate_ref[...] = u[:, :p].reshape(SUBLANES, tt, p).astype(gate_ref.dtype)
    for b in range(SUBLANES):
        for j in range(p // LANES):
            xr_ref[j, pl.ds(b, tt, stride=SUBLANES), :] = u[b * tt:(b + 1) * tt, p + j * LANES:p + (j + 1) * LANES]


def _rnn_in(xa, mod, layer, norm_g, win_p, p):
    bsz, t, d = xa.shape
    ng = bsz // SUBLANES
    tt = min(t, 64)
    ns = p // LANES
    mod_arr, mod_spec = mod.per_group()
    zero2 = lambda gi, i: (0, 0)
    return pl.pallas_call(
        functools.partial(_r1_kernel, tt=tt, p=p),
        grid=(ng, t // tt),
        in_specs=[
            pl.BlockSpec((SUBLANES, tt, d), lambda gi, i: (gi, i, 0)),
            mod_spec,
            _layer_spec((1, d), 2 * layer, zero2),
            _layer_spec((d, 2 * p), layer // 2, zero2, resident=True),
        ],
        out_specs=[
            pl.BlockSpec((SUBLANES, tt, p), lambda gi, i: (gi, i, 0)),
            pl.BlockSpec((None, ns, tt * SUBLANES, LANES), lambda gi, i: (gi, 0, i, 0)),
        ],
        out_shape=[
            jax.ShapeDtypeStruct((bsz, t, p), _STORE_DTYPE),
            jax.ShapeDtypeStruct((ng, ns, t * SUBLANES, LANES), F32),
        ],
        compiler_params=_params("parallel", "parallel"),
        name="rnn_in_proj",
    )(xa, mod_arr, norm_g, win_p)


SCAN_CHUNK = 256


def _softplus(z):
    return jnp.maximum(z, 0.0) + jnp.log(1.0 + jnp.exp(-jnp.abs(z)))


def _pack_bf16_pair(hi, lo):
    hb = lax.bitcast_convert_type(hi.astype(jnp.bfloat16).astype(F32), jnp.uint32)
    lb = lax.bitcast_convert_type(lo.astype(jnp.bfloat16).astype(F32), jnp.uint32)
    return hb | (lb >> 16)


def _unpack_bf16_pair(w):
    hi = lax.bitcast_convert_type(w & jnp.uint32(0xFFFF0000), F32)
    lo = lax.bitcast_convert_type(w << 16, F32)
    return hi, lo


def _r2_kernel(xm_ref, xp_ref, xn_ref, cw_ref, cb_ref, wg_ref, bg_ref, lam_ref, h0f_ref, h0b_ref,
               hs_ref, hff_ref, hfb_ref, ext_ref, a_ref, b_ref, hfall_ref, xh_ref, st_ref, *, tt, n_t):
    k = pl.program_id(2)
    rows = tt * SUBLANES
    halo_l = (CONV_W // 2) * SUBLANES
    halo_r = (CONV_W - 1 - CONV_W // 2) * SUBLANES
    gw = MXU_DIM

    def compute(fwd_job):
        c = jnp.where(k < n_t, k, 2 * n_t - 1 - k)
        crow = pl.ds(pl.multiple_of(c * rows, rows), rows)
        if fwd_job:
            for s in range(gw // LANES):
                ls = slice(s * LANES, (s + 1) * LANES)
                ext_ref[0:halo_l, ls] = jnp.where(c > 0, xp_ref[s], 0.0)
                ext_ref[halo_l:halo_l + rows, ls] = xm_ref[s]
                ext_ref[halo_l + rows:halo_l + rows + halo_r, ls] = jnp.where(c < n_t - 1, xn_ref[s], 0.0)
            xh = cb_ref[...] + sum(
                cw_ref[kk:kk + 1, :] * ext_ref[kk * SUBLANES:kk * SUBLANES + rows, :] for kk in range(CONV_W))
            xh_ref[crow, :] = xh
        else:
            xh = xh_ref[crow, :]
        p = _dot(xh.astype(_MXU_DTYPE), wg_ref[...]) + bg_ref[...]
        th_r = jnp.tanh(p[:, :gw])
        th_i = jnp.tanh(p[:, gw:])
        kh = (-0.5 * RG_C * math.log2(math.e)) * _softplus(-lam_ref[...])
        a = jnp.exp2(kh * th_r + kh)
        y = 1.0 - a * a
        a_ref[...] = a
        b_ref[...] = (y * lax.rsqrt(jnp.maximum(y, 1e-30))) * ((th_i + 1.0) * xh)

    def scan(fwd):
        j = k - 1
        base = (j if fwd else 2 * n_t - 1 - j) * rows
        h = st_ref[...]
        for t in (range(tt) if fwd else reversed(range(tt))):
            o = t * SUBLANES
            h = a_ref[o:o + SUBLANES, :] * h + b_ref[o:o + SUBLANES, :]
            hrow = pl.ds(pl.multiple_of(base + o, SUBLANES), SUBLANES)
            if fwd:
                hfall_ref[hrow, :] = h
            else:
                tot = hfall_ref[hrow, :] + h
                hs_ref[o:o + SUBLANES, :] = _pack_bf16_pair(tot[:, :LANES], tot[:, LANES:])
        st_ref[...] = h

    @pl.when(k == 0)
    def _():
        st_ref[...] = h0f_ref[...]
        compute(True)

    @pl.when(jnp.logical_and(k >= 1, k < n_t))
    def _():
        scan(True)
        compute(True)

    @pl.when(k == n_t)
    def _():
        scan(True)
        compute(False)
        hff_ref[...] = st_ref[...]
        st_ref[...] = h0b_ref[...]

    @pl.when(jnp.logical_and(k > n_t, k < 2 * n_t))
    def _():
        scan(False)
        compute(False)

    @pl.when(k == 2 * n_t)
    def _():
        scan(False)
        hfb_ref[...] = st_ref[...]


def _rnn_scan(xr, jr, conv_w, conv_b, wg, bg, lam, h0f, h0b, t):
    ng, ns, trows, _ = xr.shape
    p = ns * LANES
    nq = p // MXU_DIM
    spg = MXU_DIM // LANES
    assert spg == 2
    tt = min(t, SCAN_CHUNK)
    n_t = t // tt
    rows = tt * SUBLANES
    halo_l = (CONV_W // 2) * SUBLANES
    halo_r = (CONV_W - 1 - CONV_W // 2) * SUBLANES

    def chunk(k):
        return jnp.minimum(k, n_t - 1)

    def out_chunk(k):
        return jnp.where(k <= n_t, n_t - 1, 2 * n_t - k)

    def phase(k):
        return jnp.where(k < n_t, 0, 1)

    return pl.pallas_call(
        functools.partial(_r2_kernel, tt=tt, n_t=n_t),
        grid=(ng, nq, 2 * n_t + 1),
        in_specs=[
            pl.BlockSpec((None, spg, rows, LANES), lambda g, q, k: (g, q, chunk(k), 0)),
            pl.BlockSpec((None, spg, halo_l, LANES),
                         lambda g, q, k: (g, q, jnp.maximum(chunk(k) * (rows // halo_l) - 1, 0), 0)),
            pl.BlockSpec((None, spg, halo_r, LANES),
                         lambda g, q, k: (g, q, jnp.minimum((chunk(k) + 1) * (rows // halo_r), trows // halo_r - 1), 0)),
            _layer_spec((CONV_W, MXU_DIM), jr, lambda g, q, k: (0, q)),
            _layer_spec((1, MXU_DIM), jr, lambda g, q, k: (0, q)),
            _layer_spec((None, None, MXU_DIM, 2 * MXU_DIM), jr, lambda g, q, k: (phase(k), q, 0, 0)),
            _layer_spec((None, None, 1, 2 * MXU_DIM), jr, lambda g, q, k: (phase(k), q, 0, 0)),
            _layer_spec((None, 1, MXU_DIM), jr, lambda g, q, k: (phase(k), 0, q)),
            pl.BlockSpec((None, SUBLANES, MXU_DIM), lambda g, q, k: (g, 0, q)),
            pl.BlockSpec((None, SUBLANES, MXU_DIM), lambda g, q, k: (g, 0, q)),
        ],
        out_specs=[
            pl.BlockSpec((None, None, rows, LANES), lambda g, q, k: (g, q, out_chunk(k), 0)),
            pl.BlockSpec((None, SUBLANES, MXU_DIM), lambda g, q, k: (g, 0, q)),
            pl.BlockSpec((None, SUBLANES, MXU_DIM), lambda g, q, k: (g, 0, q)),
        ],
        out_shape=[
            jax.ShapeDtypeStruct((ng, nq, trows, LANES), jnp.uint32),
            jax.ShapeDtypeStruct((ng, SUBLANES, p), F32),
            jax.ShapeDtypeStruct((ng, SUBLANES, p), F32),
        ],
        scratch_shapes=[
            pltpu.VMEM((halo_l + rows + halo_r, MXU_DIM), F32),
            pltpu.VMEM((rows, MXU_DIM), F32),
            pltpu.VMEM((rows, MXU_DIM), F32),
            pltpu.VMEM((trows, MXU_DIM), F32),
            pltpu.VMEM((trows, MXU_DIM), F32),
            pltpu.VMEM((SUBLANES, MXU_DIM), F32),
        ],
        compiler_params=_params("parallel", "parallel", "arbitrary"),
        name="rnn_scan",
    )(xr, xr, xr, conv_w, conv_b, wg, bg, lam, h0f, h0b)


def _r3_mlp_kernel(x_ref, hs_ref, gate_ref, mod_ref, wo_ref, g_ref, w1_ref, b1_ref, w2_ref, b2_ref, *rest,
                   tt, p, tf, final):
    if final:
        fg_ref, o_ref, z_ref = rest
    else:
        o_ref, z_ref = rest
    for b in range(SUBLANES):
        for q in range(p // MXU_DIM):
            halves = _unpack_bf16_pair(hs_ref[q, pl.ds(b, tt, stride=SUBLANES), :])
            for s, hv in enumerate(halves):
                cols = slice(q * MXU_DIM + s * LANES, q * MXU_DIM + (s + 1) * LANES)
                gt = gate_ref[b, :, cols].astype(F32)
                z_ref[b * tt:(b + 1) * tt, cols] = (hv * _gelu_tanh(gt)).astype(z_ref.dtype)
    yx = _dot(z_ref[...], wo_ref[...])
    d = yx.shape[-1]
    m = mod_ref[...]
    x1 = x_ref[...] + m[:, 2:3, :] * yx.reshape(SUBLANES, tt, d)
    h = _norm_mod(x1, g_ref[...], m[:, 3:4, :], m[:, 4:5, :]).reshape(SUBLANES * tt, d).astype(_MXU_DTYPE)
    dff = w1_ref.shape[1]
    acc = jnp.zeros((SUBLANES * tt, d), F32)
    for j in range(dff // tf):
        a = _dot(h, w1_ref[:, j * tf:(j + 1) * tf]) + b1_ref[:, j * tf:(j + 1) * tf]
        a = jnp.maximum(a, 0.0)
        a = a * a
        acc = acc + _dot(a.astype(_MXU_DTYPE), w2_ref[j * tf:(j + 1) * tf, :])
    out = x1 + m[:, 5:6, :] * (acc + b2_ref[...]).reshape(SUBLANES, tt, d)
    if final:
        ms = jnp.mean(out * out, axis=-1, keepdims=True)
        out = out * lax.rsqrt(ms + EPS) * fg_ref[...]
    o_ref[...] = out


def _rnn_out_mlp(xa, hs, gate, mod, layer, wout_p, norm_g, w1, b1, w2, b2, final_g=None):
    bsz, t, d = xa.shape
    mod_arr, mod_spec = mod.per_group()
    p = gate.shape[-1]
    dff = w1.shape[-1]
    ng = bsz // SUBLANES
    nq = p // MXU_DIM
    tt = min(t, 64)
    tf = min(dff, 1024)
    final = final_g is not None
    zero2 = lambda gi, i: (0, 0)
    in_specs = [
        pl.BlockSpec((SUBLANES, tt, d), lambda gi, i: (gi, i, 0)),
        pl.BlockSpec((None, nq, tt * SUBLANES, LANES), lambda gi, i: (gi, 0, i, 0)),
        pl.BlockSpec((SUBLANES, tt, p), lambda gi, i: (gi, i, 0)),
        mod_spec,
        _layer_spec((p, d), layer // 2, zero2, resident=True),
        _layer_spec((1, d), 2 * layer + 1, zero2),
        _layer_spec((d, dff), layer, zero2, resident=True),
        _layer_spec((1, dff), layer, zero2),
        _layer_spec((dff, d), layer, zero2, resident=True),
        _layer_spec((1, d), layer, zero2),
    ]
    args = [xa, hs, gate, mod_arr, wout_p, norm_g, w1, b1, w2, b2]
    if final:
        in_specs.append(pl.BlockSpec((1, d), zero2))
        args.append(final_g.reshape(1, d))
    return pl.pallas_call(
        functools.partial(_r3_mlp_kernel, tt=tt, p=p, tf=tf, final=final),
        grid=(ng, t // tt),
        in_specs=in_specs,
        out_specs=pl.BlockSpec((SUBLANES, tt, d), lambda gi, i: (gi, i, 0)),
        out_shape=jax.ShapeDtypeStruct(xa.shape, F32),
        scratch_shapes=[pltpu.VMEM((SUBLANES * tt, p), _MXU_DTYPE)],
        compiler_params=_params("parallel", "parallel"),
        name="rnn_out_mlp",
    )(*args)


def _rnn_weights(lay, w_in, conv_w, conv_b, w_a, b_a, w_i, b_i, lam, w_out):
    nr = w_in.shape[0]
    d_rnn = conv_b.shape[-1]
    w_in = w_in.astype(_MXU_DTYPE)
    win_p = jnp.concatenate([lay.pad_last(w_in[..., :d_rnn]), lay.pad_last(w_in[..., d_rnn:])], axis=-1)
    cw_p = 0.5 * lay.pad_last(conv_w)
    cb_p = 0.5 * lay.pad_last(conv_b).reshape(nr, 1, lay.p)
    wg = jnp.concatenate([lay.block_diag(w_a), lay.block_diag(w_i)], axis=-1).astype(_MXU_DTYPE)
    bg = 0.5 * jnp.concatenate([lay.pad_last(b_a).reshape(nr, 2, lay.n_groups, 1, MXU_DIM),
                                lay.pad_last(b_i).reshape(nr, 2, lay.n_groups, 1, MXU_DIM)], axis=-1)
    lam_p = lay.pad_last(lam).reshape(nr, 2, 1, lay.p)
    wout_p = lay.pad_rows(w_out.astype(_MXU_DTYPE))
    return win_p, cw_p, cb_p, wg, bg, lam_p, wout_p


def kernel(x, c, ctx, c_ctx, w_mod, b_mod, norm_g, w_fourier, w_rnn_in, conv_w, conv_b,
           w_a, b_a, w_i, b_i, lam, w_rnn_out, w1, b1, w2, b2, final_g):
    bsz, seq, d = x.shape
    depth = w_mod.shape[0]
    assert bsz % SUBLANES == 0 and d % (N_FOURIER_GROUPS * LANES) == 0
    lay = _RnnLayout(w_a.shape[2], w_a.shape[-1])
    ng = bsz // SUBLANES

    pad_rows = (-(bsz + 1)) % SUBLANES
    cv = jnp.concatenate([c, c_ctx[None, :], jnp.zeros((pad_rows, d), F32)], axis=0)
    mod = _modulation(cv, w_mod, b_mod).reshape(depth, bsz + 1 + pad_rows, N_MOD, d)
    mod_c8 = jnp.broadcast_to(mod[:, bsz:bsz + 1], (depth, SUBLANES, N_MOD, d))

    w1b, w2b = w1.astype(_MXU_DTYPE), w2.astype(_MXU_DTYPE)
    wfb = w_fourier.astype(_MXU_DTYPE)
    ng2 = norm_g.reshape(depth * 2, 1, d)
    b1r, b2r = b1.reshape(depth, 1, -1), b2.reshape(depth, 1, d)
    win_p, cw_p, cb_p, wg, bg, lam_p, wout_p = _rnn_weights(
        lay, w_rnn_in, conv_w, conv_b, w_a, b_a, w_i, b_i, lam, w_rnn_out)
    zeros = jnp.zeros((ng, SUBLANES, lay.p), F32)

    for i in range(depth):
        last = i == depth - 1
        mx = _Mod(mod, mod_c8, i, False, bsz)
        mc = _Mod(mod, mod_c8, i, True, bsz)
        if i % 2 == 0:
            x = _fourier_mix_residual(x, mx, i, ng2, wfb)
            if not last:
                ctx = _fourier_mix_residual(ctx, mc, i, ng2, wfb)
        else:
            jr = i // 2
            gate_c, xr_c = _rnn_in(ctx, mc, i, ng2, win_p, lay.p)
            hs_c, hf_c, hb_c = _rnn_scan(xr_c, jr, cw_p, cb_p, wg, bg, lam_p, zeros, zeros, ctx.shape[1])
            gate_x, xr_x = _rnn_in(x, mx, i, ng2, win_p, lay.p)
            hs_x, _, _ = _rnn_scan(xr_x, jr, cw_p, cb_p, wg, bg, lam_p, hf_c, hb_c, seq)
            x = _rnn_out_mlp(x, hs_x, gate_x, mx, i, wout_p, ng2, w1b, b1r, w2b, b2r, final_g if last else None)
            if not last:
                ctx = _rnn_out_mlp(ctx, hs_c, gate_c, mc, i, wout_p, ng2, w1b, b1r, w2b, b2r)
            continue
        x = _mlp(x, mx, i, ng2, w1b, b1r, w2b, b2r, final_g if last else None)
        if not last:
            ctx = _mlp(ctx, mc, i, ng2, w1b, b1r, w2b, b2r)
    return x
```

```python
import functools
import math

import numpy as np
import jax
import jax.numpy as jnp
from jax import lax
from jax.experimental import pallas as pl
from jax.experimental.pallas import tpu as pltpu

EPS = 1e-6
RG_C = 8.0
N_FOURIER_GROUPS = 4
CONV_W = 4
N_MOD = 6

LANES = 128
SUBLANES = 8
MXU_DIM = 256
VMEM_LIMIT = 56 << 20
_MXU_DTYPE = jnp.bfloat16
_STORE_DTYPE = jnp.bfloat16
F32 = jnp.float32


def _params(*sem):
    return pltpu.CompilerParams(dimension_semantics=sem, vmem_limit_bytes=VMEM_LIMIT)


def _resident(shape, index_map):
    return pl.BlockSpec(shape, index_map, pipeline_mode=pl.Buffered(1))


def _dot(a, b):
    return jnp.dot(a, b, preferred_element_type=F32)


def _layer_spec(block, layer, index_map, resident=False):
    make = _resident if resident else pl.BlockSpec
    return make((None,) + tuple(block), lambda *ids: (layer,) + tuple(index_map(*ids)))


class _Mod:
    def __init__(self, table, table_ctx8, layer, is_ctx, bsz):
        self.table, self.table_ctx8, self.layer, self.is_ctx, self.bsz = table, table_ctx8, layer, is_ctx, bsz

    def per_batch(self):
        d = self.table.shape[-1]
        layer, fixed = self.layer, (self.bsz if self.is_ctx else None)
        return self.table, pl.BlockSpec(
            (None, None, N_MOD, d), lambda b, i: (layer, b if fixed is None else fixed, 0, 0))

    def per_group(self):
        d = self.table.shape[-1]
        layer = self.layer
        if self.is_ctx:
            return self.table_ctx8, pl.BlockSpec((None, SUBLANES, N_MOD, d), lambda gi, i: (layer, 0, 0, 0))
        return self.table, pl.BlockSpec((None, SUBLANES, N_MOD, d), lambda gi, i: (layer, gi, 0, 0))


def _norm_mod(x, g, shift, scale):
    ms = jnp.mean(x * x, axis=-1, keepdims=True)
    y = x * lax.rsqrt(ms + EPS) * g
    return y * (1.0 + scale) + shift


def _mod_kernel(cv_ref, w_ref, b_ref, o_ref):
    c = cv_ref[...]
    s = c * jax.nn.sigmoid(c)
    o_ref[...] = _dot(s.astype(_MXU_DTYPE), w_ref[...].astype(_MXU_DTYPE)) + b_ref[...]


def _modulation(cv, w_mod, b_mod):
    depth, d, n = w_mod.shape
    rows = cv.shape[0]
    tn = min(n, 1024)
    return pl.pallas_call(
        _mod_kernel,
        grid=(depth, n // tn),
        in_specs=[
            pl.BlockSpec((rows, d), lambda i, j: (0, 0)),
            pl.BlockSpec((None, d, tn), lambda i, j: (i, 0, j)),
            pl.BlockSpec((None, 1, tn), lambda i, j: (i, 0, j)),
        ],
        out_specs=pl.BlockSpec((None, rows, tn), lambda i, j: (i, 0, j)),
        out_shape=jax.ShapeDtypeStruct((depth, rows, n), F32),
        compiler_params=_params("parallel", "parallel"),
        name="modulation",
    )(cv, w_mod, b_mod.reshape(depth, 1, n))


MLP_ROWS = 1024


def _mlp_kernel(x_ref, mod_ref, g_ref, w1_ref, b1_ref, w2_ref, b2_ref, *rest, tf, final):
    if final:
        fg_ref, o_ref = rest
    else:
        (o_ref,) = rest
    x = x_ref[...]
    h = _norm_mod(x, g_ref[...], mod_ref[3:4, :], mod_ref[4:5, :]).astype(_MXU_DTYPE)
    dff = w1_ref.shape[1]
    acc = jnp.zeros(x.shape, F32)
    for j in range(dff // tf):
        a = _dot(h, w1_ref[:, j * tf:(j + 1) * tf]) + b1_ref[:, j * tf:(j + 1) * tf]
        a = jnp.maximum(a, 0.0)
        a = a * a
        acc = acc + _dot(a.astype(_MXU_DTYPE), w2_ref[j * tf:(j + 1) * tf, :])
    out = x + mod_ref[5:6, :] * (acc + b2_ref[...])
    if final:
        ms = jnp.mean(out * out, axis=-1, keepdims=True)
        out = out * lax.rsqrt(ms + EPS) * fg_ref[...]
    o_ref[...] = out


def _mlp(xa, mod, layer, norm_g, w1, b1, w2, b2, final_g=None):
    bsz, t, d = xa.shape
    dff = w1.shape[-1]
    tm = min(t, MLP_ROWS)
    tf = min(dff, 1024)
    final = final_g is not None
    mod_arr, mod_spec = mod.per_batch()
    zero2 = lambda b, i: (0, 0)
    in_specs = [
        pl.BlockSpec((None, tm, d), lambda b, i: (b, i, 0)),
        mod_spec,
        _layer_spec((1, d), 2 * layer + 1, zero2),
        _layer_spec((d, dff), layer, zero2, resident=True),
        _layer_spec((1, dff), layer, zero2),
        _layer_spec((dff, d), layer, zero2, resident=True),
        _layer_spec((1, d), layer, zero2),
    ]
    args = [xa, mod_arr, norm_g, w1, b1, w2, b2]
    if final:
        in_specs.append(pl.BlockSpec((1, d), zero2))
        args.append(final_g.reshape(1, d))
    return pl.pallas_call(
        functools.partial(_mlp_kernel, tf=tf, final=final),
        grid=(bsz, t // tm),
        in_specs=in_specs,
        out_specs=pl.BlockSpec((None, tm, d), lambda b, i: (b, i, 0)),
        out_shape=jax.ShapeDtypeStruct(xa.shape, F32),
        compiler_params=_params("parallel", "parallel"),
        name="mlp",
    )(*args)


def _dft_tables(t, gs):
    c = np.arange(gs, dtype=np.int64)
    ang = 2.0 * np.pi * ((c[:, None] * c[None, :]) % gs) / gs
    wg = np.concatenate([np.cos(ang), -np.sin(ang)], axis=1)
    k = np.arange(t, dtype=np.int64)
    angt = 2.0 * np.pi * ((k[:, None] * k[None, :]) % t) / t
    cs = np.stack([np.cos(angt), np.sin(angt)])
    return jnp.asarray(wg, _MXU_DTYPE), jnp.asarray(cs, _MXU_DTYPE)


def _f1_kernel(x_ref, mod_ref, g_ref, wg_ref, o_ref, *, gs):
    h = _norm_mod(x_ref[...], g_ref[...], mod_ref[0:1, :], mod_ref[1:2, :]).astype(_MXU_DTYPE)
    for q in range(h.shape[-1] // gs):
        res = _dot(h[:, q * gs:(q + 1) * gs], wg_ref[...])
        o_ref[0, :, q * gs:(q + 1) * gs] = res[:, :gs].astype(o_ref.dtype)
        o_ref[1, :, q * gs:(q + 1) * gs] = res[:, gs:].astype(o_ref.dtype)


def _f2_kernel(x_ref, g2_ref, cs_ref, mod_ref, wo_ref, o_ref, *, tm, scale):
    r0 = pl.multiple_of(pl.program_id(1) * tm, tm)
    y = _dot(cs_ref[0, pl.ds(r0, tm), :], g2_ref[0]) + _dot(cs_ref[1, pl.ds(r0, tm), :], g2_ref[1])
    yx = _dot((y * scale).astype(_MXU_DTYPE), wo_ref[...])
    o_ref[...] = x_ref[...] + mod_ref[2:3, :] * yx


def _fourier_layer(xa, mod, layer, norm_g, w_out):
    bsz, t, d = xa.shape
    mod_arr, mod_spec = mod.per_batch()
    zero2 = lambda b, i: (0, 0)
    jf = layer // 2
    gs = d // N_FOURIER_GROUPS
    tm = min(t, 512)
    wg, cs = _dft_tables(t, gs)
    g2 = pl.pallas_call(
        functools.partial(_f1_kernel, gs=gs),
        grid=(bsz, t // tm),
        in_specs=[
            pl.BlockSpec((None, tm, d), lambda b, i: (b, i, 0)),
            mod_spec,
            _layer_spec((1, d), 2 * layer, zero2),
            pl.BlockSpec((gs, 2 * gs), zero2),
        ],
        out_specs=pl.BlockSpec((None, 2, tm, d), lambda b, i: (b, 0, i, 0)),
        out_shape=jax.ShapeDtypeStruct((bsz, 2, t, d), _MXU_DTYPE),
        compiler_params=_params("parallel", "parallel"),
        name="fourier_group_dft",
    )(xa, mod_arr, norm_g, wg)
    scale = 1.0 / math.sqrt(t * gs)
    return pl.pallas_call(
        functools.partial(_f2_kernel, tm=tm, scale=scale),
        grid=(bsz, t // tm),
        in_specs=[
            pl.BlockSpec((None, tm, d), lambda b, i: (b, i, 0)),
            pl.BlockSpec((None, 2, t, d), lambda b, i: (b, 0, 0, 0)),
            _resident((2, t, t), lambda b, i: (0, 0, 0)),
            mod_spec,
            _layer_spec((d, d), jf, zero2, resident=True),
        ],
        out_specs=pl.BlockSpec((None, tm, d), lambda b, i: (b, i, 0)),
        out_shape=jax.ShapeDtypeStruct(xa.shape, F32),
        compiler_params=_params("parallel", "arbitrary"),
        name="fourier_time_dft",
    )(xa, g2, cs, mod_arr, w_out)


FFT_RADIX = 8
FFT_MIN_BLOCK = 32


def _fft_tables(t, gs, tmn):
    n2 = t // FFT_RADIX
    k2 = np.arange(n2, dtype=np.int64)
    tabs = []
    for k1 in range(FFT_RADIX):
        k = k1 + FFT_RADIX * k2
        ang = 2.0 * np.pi * ((k[:, None] * k2[None, :]) % t) / t
        tabs.append(np.concatenate([np.cos(ang), np.sin(ang)], axis=1))
    c = np.arange(gs, dtype=np.int64)
    angc = 2.0 * np.pi * ((c[:, None] * c[None, :]) % gs) / gs
    wg = np.concatenate([np.cos(angc), -np.sin(angc)], axis=1)
    runs = tmn // FFT_RADIX
    pm = np.zeros((tmn, tmn))
    m = np.arange(runs)
    for k1 in range(FFT_RADIX):
        pm[FFT_RADIX * m + k1, k1 * runs + m] = 1.0
    return (jnp.asarray(np.stack(tabs), _MXU_DTYPE), jnp.asarray(wg, _MXU_DTYPE), jnp.asarray(pm, _MXU_DTYPE))


def _dft4(v):
    (ar, ai), (br, bi), (cr, ci), (dr, di) = v
    e0r, e0i, e1r, e1i = ar + cr, ai + ci, ar - cr, ai - ci
    f0r, f0i, f1r, f1i = br + dr, bi + di, br - dr, bi - di
    return [(e0r + f0r, e0i + f0i), (e1r + f1i, e1i - f1r), (e0r - f0r, e0i - f0i), (e1r - f1i, e1i + f1r)]


def _radix8(w):
    s = [(w[i][0] + w[i + 4][0], w[i][1] + w[i + 4][1]) for i in range(4)]
    dd = [(w[i][0] - w[i + 4][0], w[i][1] - w[i + 4][1]) for i in range(4)]
    rt = math.sqrt(0.5)
    tw = [dd[0],
          ((dd[1][0] + dd[1][1]) * rt, (dd[1][1] - dd[1][0]) * rt),
          (dd[2][1], -dd[2][0]),
          ((dd[3][1] - dd[3][0]) * rt, -(dd[3][0] + dd[3][1]) * rt)]
    even, odd = _dft4(s), _dft4(tw)
    return [even[0], odd[0], even[1], odd[1], even[2], odd[2], even[3], odd[3]]


def _fft_kernel(x_ref, xt_ref, mod_ref, g_ref, tab_ref, wg_ref, pm_ref, wo_ref, o_ref, z_ref, bb_ref,
                *, t, gs, tmn, scale):
    i = pl.program_id(1)
    n2 = t // FFT_RADIX
    d = x_ref.shape[-1]

    @pl.when(i == 0)
    def _():
        rstd = []
        for t1 in range(FFT_RADIX):
            xb = x_ref[t1 * n2:(t1 + 1) * n2, :]
            rstd.append(lax.rsqrt(jnp.mean(xb * xb, axis=-1, keepdims=True) + EPS))
        for q in range(d // gs):
            cols = slice(q * gs, (q + 1) * gs)
            w = []
            for t1 in range(FFT_RADIX):
                h = ((x_ref[t1 * n2:(t1 + 1) * n2, cols] * rstd[t1] * g_ref[:, cols]) * (1.0 + mod_ref[1:2, cols])
                     + mod_ref[0:1, cols])
                wc = _dot(h.astype(_MXU_DTYPE), wg_ref[...])
                w.append((wc[:, :gs], wc[:, gs:]))
            for k1, (re, im) in enumerate(_radix8(w)):
                bb_ref[k1, 0:n2, cols] = re.astype(bb_ref.dtype)
                bb_ref[k1, n2:2 * n2, cols] = im.astype(bb_ref.dtype)
        for k1 in range(FFT_RADIX):
            y = _dot(tab_ref[k1], bb_ref[k1])
            z_ref[k1 * n2:(k1 + 1) * n2, :] = (y * scale).astype(z_ref.dtype)

    runs = tmn // FFT_RADIX
    zs = jnp.concatenate(
        [z_ref[pl.ds(pl.multiple_of(k1 * n2 + i * runs, runs), runs), :] for k1 in range(FFT_RADIX)], axis=0)
    zn = _dot(pm_ref[...], zs).astype(_MXU_DTYPE)
    yx = _dot(zn, wo_ref[...])
    o_ref[...] = xt_ref[...] + mod_ref[2:3, :] * yx


def _fourier_layer_fft(xa, mod, layer, norm_g, w_out):
    bsz, t, d = xa.shape
    mod_arr, mod_spec = mod.per_batch()
    zero2 = lambda b, i: (0, 0)
    gs = d // N_FOURIER_GROUPS
    tmn = min(t, 512)
    tab, wg, pm = _fft_tables(t, gs, tmn)
    n2 = t // FFT_RADIX
    scale = 1.0 / math.sqrt(t * gs)
    return pl.pallas_call(
        functools.partial(_fft_kernel, t=t, gs=gs, tmn=tmn, scale=scale),
        grid=(bsz, t // tmn),
        in_specs=[
            pl.BlockSpec((None, t, d), lambda b, i: (jnp.minimum(b + jnp.minimum(i, 1), bsz - 1), 0, 0)),
            pl.BlockSpec((None, tmn, d), lambda b, i: (b, i, 0)),
            mod_spec,
            _layer_spec((1, d), 2 * layer, zero2),
            _resident((FFT_RADIX, n2, 2 * n2), lambda b, i: (0, 0, 0)),
            _resident((gs, 2 * gs), zero2),
            _resident((tmn, tmn), zero2),
            _layer_spec((d, d), layer // 2, zero2, resident=True),
        ],
        out_specs=pl.BlockSpec((None, tmn, d), lambda b, i: (b, i, 0)),
        out_shape=jax.ShapeDtypeStruct(xa.shape, F32),
        scratch_shapes=[pltpu.VMEM((t, d), _MXU_DTYPE), pltpu.VMEM((FFT_RADIX, 2 * n2, d), _MXU_DTYPE)],
        compiler_params=_params("parallel", "arbitrary"),
        name="fourier_fft",
    )(xa, xa, mod_arr, norm_g, tab, wg, pm, w_out)


def _fourier_mix_residual(xa, mod, layer, norm_g, w_out):
    t = xa.shape[1]
    n2 = t // FFT_RADIX
    if t % FFT_RADIX == 0 and n2 >= FFT_MIN_BLOCK and n2 % 16 == 0:
        return _fourier_layer_fft(xa, mod, layer, norm_g, w_out)
    return _fourier_layer(xa, mod, layer, norm_g, w_out)


class _RnnLayout:
    def __init__(self, n_blocks, rb):
        self.rb = rb
        self.n_blocks = n_blocks
        self.per_group = MXU_DIM // rb
        self.n_groups = -(-n_blocks // self.per_group)
        self.p = self.n_groups * MXU_DIM

    def pad_last(self, v):
        pg, rb = self.per_group, self.rb
        lead = v.shape[:-1]
        nl = len(lead)
        vb = v.reshape(lead + (self.n_blocks, rb))
        vb = jnp.pad(vb, [(0, 0)] * nl + [(0, self.n_groups * pg - self.n_blocks), (0, 0)])
        vg = vb.reshape(lead + (self.n_groups, pg * rb))
        vg = jnp.pad(vg, [(0, 0)] * nl + [(0, 0), (0, MXU_DIM - pg * rb)])
        return vg.reshape(lead + (self.p,))

    def pad_rows(self, w):
        pg, rb = self.per_group, self.rb
        lead, d = w.shape[:-2], w.shape[-1]
        nl = len(lead)
        wb = w.reshape(lead + (self.n_blocks, rb, d))
        wb = jnp.pad(wb, [(0, 0)] * nl + [(0, self.n_groups * pg - self.n_blocks), (0, 0), (0, 0)])
        wg = wb.reshape(lead + (self.n_groups, pg * rb, d))
        wg = jnp.pad(wg, [(0, 0)] * nl + [(0, 0), (0, MXU_DIM - pg * rb), (0, 0)])
        return wg.reshape(lead + (self.p, d))

    def block_diag(self, w):
        pg, rb = self.per_group, self.rb
        lead = w.shape[:-3]
        nl = len(lead)
        tot = self.n_groups * pg
        wp = jnp.pad(w, [(0, 0)] * nl + [(0, tot - self.n_blocks), (0, 0), (0, 0)])
        wp = wp.reshape(lead + (self.n_groups, pg, rb, 1, rb))
        eye = jnp.eye(pg, dtype=w.dtype).reshape(pg, 1, pg, 1)
        m = (wp * eye).reshape(lead + (self.n_groups, pg * rb, pg * rb))
        extra = MXU_DIM - pg * rb
        return jnp.pad(m, [(0, 0)] * (nl + 1) + [(0, extra), (0, extra)])


def _gelu_tanh(x):
    return 0.5 * x * (1.0 + jnp.tanh(math.sqrt(2.0 / math.pi) * (x + 0.044715 * (x * x * x))))


def _r1_kernel(x_ref, mod_ref, g_ref, win_ref, gate_ref, xr_ref, *, tt, p):
    x = x_ref[...]
    m = mod_ref[...]
    h = _norm_mod(x, g_ref[...], m[:, 0:1, :], m[:, 1:2, :])
    hb = h.reshape(SUBLANES * tt, h.shape[-1]).astype(_MXU_DTYPE)
    u = _dot(hb, win_ref[...])
    gate_ref[...] = u[:, :p].reshape(SUBLANES, tt, p).astype(gate_ref.dtype)
    for b in range(SUBLANES):
        for j in range(p // LANES):
            xr_ref[j, pl.ds(b, tt, stride=SUBLANES), :] = u[b * tt:(b + 1) * tt, p + j * LANES:p + (j + 1) * LANES]


def _rnn_in(xa, mod, layer, norm_g, win_p, p):
    bsz, t, d = xa.shape
    ng = bsz // SUBLANES
    tt = min(t, 64)
    ns = p // LANES
    mod_arr, mod_spec = mod.per_group()
    zero2 = lambda gi, i: (0, 0)
    return pl.pallas_call(
        functools.partial(_r1_kernel, tt=tt, p=p),
        grid=(ng, t // tt),
        in_specs=[
            pl.BlockSpec((SUBLANES, tt, d), lambda gi, i: (gi, i, 0)),
            mod_spec,
            _layer_spec((1, d), 2 * layer, zero2),
            _layer_spec((d, 2 * p), layer // 2, zero2, resident=True),
        ],
        out_specs=[
            pl.BlockSpec((SUBLANES, tt, p), lambda gi, i: (gi, i, 0)),
            pl.BlockSpec((None, ns, tt * SUBLANES, LANES), lambda gi, i: (gi, 0, i, 0)),
        ],
        out_shape=[
            jax.ShapeDtypeStruct((bsz, t, p), _STORE_DTYPE),
            jax.ShapeDtypeStruct((ng, ns, t * SUBLANES, LANES), F32),
        ],
        compiler_params=_params("parallel", "parallel"),
        name="rnn_in_proj",
    )(xa, mod_arr, norm_g, win_p)


SCAN_CHUNK = 256


def _softplus(z):
    return jnp.maximum(z, 0.0) + jnp.log(1.0 + jnp.exp(-jnp.abs(z)))


def _pack_bf16_pair(hi, lo):
    hb = lax.bitcast_convert_type(hi.astype(jnp.bfloat16).astype(F32), jnp.uint32)
    lb = lax.bitcast_convert_type(lo.astype(jnp.bfloat16).astype(F32), jnp.uint32)
    return hb | (lb >> 16)


def _unpack_bf16_pair(w):
    hi = lax.bitcast_convert_type(w & jnp.uint32(0xFFFF0000), F32)
    lo = lax.bitcast_convert_type(w << 16, F32)
    return hi, lo


def _r2_kernel(xm_ref, xp_ref, xn_ref, cw_ref, cb_ref, wg_ref, bg_ref, lam_ref, h0f_ref, h0b_ref,
               hs_ref, hff_ref, hfb_ref, ext_ref, a_ref, b_ref, hfall_ref, xh_ref, st_ref, *, tt, n_t):
    k = pl.program_id(2)
    rows = tt * SUBLANES
    halo_l = (CONV_W // 2) * SUBLANES
    halo_r = (CONV_W - 1 - CONV_W // 2) * SUBLANES
    gw = MXU_DIM

    def compute(fwd_job):
        c = jnp.where(k < n_t, k, 2 * n_t - 1 - k)
        crow = pl.ds(pl.multiple_of(c * rows, rows), rows)
        if fwd_job:
            for s in range(gw // LANES):
                ls = slice(s * LANES, (s + 1) * LANES)
                ext_ref[0:halo_l, ls] = jnp.where(c > 0, xp_ref[s], 0.0)
                ext_ref[halo_l:halo_l + rows, ls] = xm_ref[s]
                ext_ref[halo_l + rows:halo_l + rows + halo_r, ls] = jnp.where(c < n_t - 1, xn_ref[s], 0.0)
            xh = cb_ref[...] + sum(
                cw_ref[kk:kk + 1, :] * ext_ref[kk * SUBLANES:kk * SUBLANES + rows, :] for kk in range(CONV_W))
            xh_ref[crow, :] = xh
        else:
            xh = xh_ref[crow, :]
        p = _dot(xh.astype(_MXU_DTYPE), wg_ref[...]) + bg_ref[...]
        th_r = jnp.tanh(p[:, :gw])
        th_i = jnp.tanh(p[:, gw:])
        kh = (-0.5 * RG_C * math.log2(math.e)) * _softplus(-lam_ref[...])
        a = jnp.exp2(kh * th_r + kh)
        y = 1.0 - a * a
        a_ref[...] = a
        b_ref[...] = (y * lax.rsqrt(jnp.maximum(y, 1e-30))) * ((th_i + 1.0) * xh)

    def scan(fwd):
        j = k - 1
        base = (j if fwd else 2 * n_t - 1 - j) * rows
        h = st_ref[...]
        for t in (range(tt) if fwd else reversed(range(tt))):
            o = t * SUBLANES
            h = a_ref[o:o + SUBLANES, :] * h + b_ref[o:o + SUBLANES, :]
            hrow = pl.ds(pl.multiple_of(base + o, SUBLANES), SUBLANES)
            if fwd:
                hfall_ref[hrow, :] = h
            else:
                tot = hfall_ref[hrow, :] + h
                hs_ref[o:o + SUBLANES, :] = _pack_bf16_pair(tot[:, :LANES], tot[:, LANES:])
        st_ref[...] = h

    @pl.when(k == 0)
    def _():
        st_ref[...] = h0f_ref[...]
        compute(True)

    @pl.when(jnp.logical_and(k >= 1, k < n_t))
    def _():
        scan(True)
        compute(True)

    @pl.when(k == n_t)
    def _():
        scan(True)
        compute(False)
        hff_ref[...] = st_ref[...]
        st_ref[...] = h0b_ref[...]

    @pl.when(jnp.logical_and(k > n_t, k < 2 * n_t))
    def _():
        scan(False)
        compute(False)

    @pl.when(k == 2 * n_t)
    def _():
        scan(False)
        hfb_ref[...] = st_ref[...]


def _rnn_scan(xr, jr, conv_w, conv_b, wg, bg, lam, h0f, h0b, t):
    ng, ns, trows, _ = xr.shape
    p = ns * LANES
    nq = p // MXU_DIM
    spg = MXU_DIM // LANES
    assert spg == 2
    tt = min(t, SCAN_CHUNK)
    n_t = t // tt
    rows = tt * SUBLANES
    halo_l = (CONV_W // 2) * SUBLANES
    halo_r = (CONV_W - 1 - CONV_W // 2) * SUBLANES

    def chunk(k):
        return jnp.minimum(k, n_t - 1)

    def out_chunk(k):
        return jnp.where(k <= n_t, n_t - 1, 2 * n_t - k)

    def phase(k):
        return jnp.where(k < n_t, 0, 1)

    return pl.pallas_call(
        functools.partial(_r2_kernel, tt=tt, n_t=n_t),
        grid=(ng, nq, 2 * n_t + 1),
        in_specs=[
            pl.BlockSpec((None, spg, rows, LANES), lambda g, q, k: (g, q, chunk(k), 0)),
            pl.BlockSpec((None, spg, halo_l, LANES),
                         lambda g, q, k: (g, q, jnp.maximum(chunk(k) * (rows // halo_l) - 1, 0), 0)),
            pl.BlockSpec((None, spg, halo_r, LANES),
                         lambda g, q, k: (g, q, jnp.minimum((chunk(k) + 1) * (rows // halo_r), trows // halo_r - 1), 0)),
            _layer_spec((CONV_W, MXU_DIM), jr, lambda g, q, k: (0, q)),
            _layer_spec((1, MXU_DIM), jr, lambda g, q, k: (0, q)),
            _layer_spec((None, None, MXU_DIM, 2 * MXU_DIM), jr, lambda g, q, k: (phase(k), q, 0, 0)),
            _layer_spec((None, None, 1, 2 * MXU_DIM), jr, lambda g, q, k: (phase(k), q, 0, 0)),
            _layer_spec((None, 1, MXU_DIM), jr, lambda g, q, k: (phase(k), 0, q)),
            pl.BlockSpec((None, SUBLANES, MXU_DIM), lambda g, q, k: (g, 0, q)),
            pl.BlockSpec((None, SUBLANES, MXU_DIM), lambda g, q, k: (g, 0, q)),
        ],
        out_specs=[
            pl.BlockSpec((None, None, rows, LANES), lambda g, q, k: (g, q, out_chunk(k), 0)),
            pl.BlockSpec((None, SUBLANES, MXU_DIM), lambda g, q, k: (g, 0, q)),
            pl.BlockSpec((None, SUBLANES, MXU_DIM), lambda g, q, k: (g, 0, q)),
        ],
        out_shape=[
            jax.ShapeDtypeStruct((ng, nq, trows, LANES), jnp.uint32),
            jax.ShapeDtypeStruct((ng, SUBLANES, p), F32),
            jax.ShapeDtypeStruct((ng, SUBLANES, p), F32),
        ],
        scratch_shapes=[
            pltpu.VMEM((halo_l + rows + halo_r, MXU_DIM), F32),
            pltpu.VMEM((rows, MXU_DIM), F32),
            pltpu.VMEM((rows, MXU_DIM), F32),
            pltpu.VMEM((trows, MXU_DIM), F32),
            pltpu.VMEM((trows, MXU_DIM), F32),
            pltpu.VMEM((SUBLANES, MXU_DIM), F32),
        ],
        compiler_params=_params("parallel", "parallel", "arbitrary"),
        name="rnn_scan",
    )(xr, xr, xr, conv_w, conv_b, wg, bg, lam, h0f, h0b)


def _r3_mlp_kernel(x_ref, hs_ref, gate_ref, mod_ref, wo_ref, g_ref, w1_ref, b1_ref, w2_ref, b2_ref, *rest,
                   tt, p, tf, final):
    if final:
        fg_ref, o_ref, z_ref = rest
    else:
        o_ref, z_ref = rest
    for b in range(SUBLANES):
        for q in range(p // MXU_DIM):
            halves = _unpack_bf16_pair(hs_ref[q, pl.ds(b, tt, stride=SUBLANES), :])
            for s, hv in enumerate(halves):
                cols = slice(q * MXU_DIM + s * LANES, q * MXU_DIM + (s + 1) * LANES)
                gt = gate_ref[b, :, cols].astype(F32)
                z_ref[b * tt:(b + 1) * tt, cols] = (hv * _gelu_tanh(gt)).astype(z_ref.dtype)
    yx = _dot(z_ref[...], wo_ref[...])
    d = yx.shape[-1]
    m = mod_ref[...]
    x1 = x_ref[...] + m[:, 2:3, :] * yx.reshape(SUBLANES, tt, d)
    h = _norm_mod(x1, g_ref[...], m[:, 3:4, :], m[:, 4:5, :]).reshape(SUBLANES * tt, d).astype(_MXU_DTYPE)
    dff = w1_ref.shape[1]
    acc = jnp.zeros((SUBLANES * tt, d), F32)
    for j in range(dff // tf):
        a = _dot(h, w1_ref[:, j * tf:(j + 1) * tf]) + b1_ref[:, j * tf:(j + 1) * tf]
        a = jnp.maximum(a, 0.0)
        a = a * a
        acc = acc + _dot(a.astype(_MXU_DTYPE), w2_ref[j * tf:(j + 1) * tf, :])
    out = x1 + m[:, 5:6, :] * (acc + b2_ref[...]).reshape(SUBLANES, tt, d)
    if final:
        ms = jnp.mean(out * out, axis=-1, keepdims=True)
        out = out * lax.rsqrt(ms + EPS) * fg_ref[...]
    o_ref[...] = out


def _rnn_out_mlp(xa, hs, gate, mod, layer, wout_p, norm_g, w1, b1, w2, b2, final_g=None):
    bsz, t, d = xa.shape
    mod_arr, mod_spec = mod.per_group()
    p = gate.shape[-1]
    dff = w1.shape[-1]
    ng = bsz // SUBLANES
    nq = p // MXU_DIM
    tt = min(t, 64)
    tf = min(dff, 1024)
    final = final_g is not None
    zero2 = lambda gi, i: (0, 0)
    in_specs = [
        pl.BlockSpec((SUBLANES, tt, d), lambda gi, i: (gi, i, 0)),
        pl.BlockSpec((None, nq, tt * SUBLANES, LANES), lambda gi, i: (gi, 0, i, 0)),
        pl.BlockSpec((SUBLANES, tt, p), lambda gi, i: (gi, i, 0)),
        mod_spec,
        _layer_spec((p, d), layer // 2, zero2, resident=True),
        _layer_spec((1, d), 2 * layer + 1, zero2),
        _layer_spec((d, dff), layer, zero2, resident=True),
        _layer_spec((1, dff), layer, zero2),
        _layer_spec((dff, d), layer, zero2, resident=True),
        _layer_spec((1, d), layer, zero2),
    ]
    args = [xa, hs, gate, mod_arr, wout_p, norm_g, w1, b1, w2, b2]
    if final:
        in_specs.append(pl.BlockSpec((1, d), zero2))
        args.append(final_g.reshape(1, d))
    return pl.pallas_call(
        functools.partial(_r3_mlp_kernel, tt=tt, p=p, tf=tf, final=final),
        grid=(ng, t // tt),
        in_specs=in_specs,
        out_specs=pl.BlockSpec((SUBLANES, tt, d), lambda gi, i: (gi, i, 0)),
        out_shape=jax.ShapeDtypeStruct(xa.shape, F32),
        scratch_shapes=[pltpu.VMEM((SUBLANES * tt, p), _MXU_DTYPE)],
        compiler_params=_params("parallel", "parallel"),
        name="rnn_out_mlp",
    )(*args)


def _rnn_weights(lay, w_in, conv_w, conv_b, w_a, b_a, w_i, b_i, lam, w_out):
    nr = w_in.shape[0]
    d_rnn = conv_b.shape[-1]
    w_in = w_in.astype(_MXU_DTYPE)
    win_p = jnp.concatenate([lay.pad_last(w_in[..., :d_rnn]), lay.pad_last(w_in[..., d_rnn:])], axis=-1)
    cw_p = 0.5 * lay.pad_last(conv_w)
    cb_p = 0.5 * lay.pad_last(conv_b).reshape(nr, 1, lay.p)
    wg = jnp.concatenate([lay.block_diag(w_a), lay.block_diag(w_i)], axis=-1).astype(_MXU_DTYPE)
    bg = 0.5 * jnp.concatenate([lay.pad_last(b_a).reshape(nr, 2, lay.n_groups, 1, MXU_DIM),
                                lay.pad_last(b_i).reshape(nr, 2, lay.n_groups, 1, MXU_DIM)], axis=-1)
    lam_p = lay.pad_last(lam).reshape(nr, 2, 1, lay.p)
    wout_p = lay.pad_rows(w_out.astype(_MXU_DTYPE))
    return win_p, cw_p, cb_p, wg, bg, lam_p, wout_p


def kernel(x, c, ctx, c_ctx, w_mod, b_mod, norm_g, w_fourier, w_rnn_in, conv_w, conv_b,
           w_a, b_a, w_i, b_i, lam, w_rnn_out, w1, b1, w2, b2, final_g):
    bsz, seq, d = x.shape
    depth = w_mod.shape[0]
    assert bsz % SUBLANES == 0 and d % (N_FOURIER_GROUPS * LANES) == 0
    lay = _RnnLayout(w_a.shape[2], w_a.shape[-1])
    ng = bsz // SUBLANES

    pad_rows = (-(bsz + 1)) % SUBLANES
    cv = jnp.concatenate([c, c_ctx[None, :], jnp.zeros((pad_rows, d), F32)], axis=0)
    mod = _modulation(cv, w_mod, b_mod).reshape(depth, bsz + 1 + pad_rows, N_MOD, d)
    mod_c8 = jnp.broadcast_to(mod[:, bsz:bsz + 1], (depth, SUBLANES, N_MOD, d))

    w1b, w2b = w1.astype(_MXU_DTYPE), w2.astype(_MXU_DTYPE)
    wfb = w_fourier.astype(_MXU_DTYPE)
    ng2 = norm_g.reshape(depth * 2, 1, d)
    b1r, b2r = b1.reshape(depth, 1, -1), b2.reshape(depth, 1, d)
    win_p, cw_p, cb_p, wg, bg, lam_p, wout_p = _rnn_weights(
        lay, w_rnn_in, conv_w, conv_b, w_a, b_a, w_i, b_i, lam, w_rnn_out)
    zeros = jnp.zeros((ng, SUBLANES, lay.p), F32)

    for i in range(depth):
        last = i == depth - 1
        mx = _Mod(mod, mod_c8, i, False, bsz)
        mc = _Mod(mod, mod_c8, i, True, bsz)
        if i % 2 == 0:
            x = _fourier_mix_residual(x, mx, i, ng2, wfb)
            if not last:
                ctx = _fourier_mix_residual(ctx, mc, i, ng2, wfb)
        else:
            jr = i // 2
            gate_c, xr_c = _rnn_in(ctx, mc, i, ng2, win_p, lay.p)
            hs_c, hf_c, hb_c = _rnn_scan(xr_c, jr, cw_p, cb_p, wg, bg, lam_p, zeros, zeros, ctx.shape[1])
            gate_x, xr_x = _rnn_in(x, mx, i, ng2, win_p, lay.p)
            hs_x, _, _ = _rnn_scan(xr_x, jr, cw_p, cb_p, wg, bg, lam_p, hf_c, hb_c, seq)
            x = _rnn_out_mlp(x, hs_x, gate_x, mx, i, wout_p, ng2, w1b, b1r, w2b, b2r, final_g if last else None)
            if not last:
                ctx = _rnn_out_mlp(ctx, hs_c, gate_c, mc, i, wout_p, ng2, w1b, b1r, w2b, b2r)
            continue
        x = _mlp(x, mx, i, ng2, w1b, b1r, w2b, b2r, final_g if last else None)
        if not last:
            ctx = _mlp(ctx, mc, i, ng2, w1b, b1r, w2b, b2r)
    return x
```
